```python
import math
import jax, jax.numpy as jnp
from jax import lax
import numpy as np

D_MODEL = 2048
BATCH = 4
SEQ = 4096
DEPTH = 2

N_A_LAYERS = DEPTH // 2
N_B_LAYERS = DEPTH - N_A_LAYERS
D_FF = 5632
PLE_DIM = 256
EPS = 1e-6
N_NORMS = 8
NEG = -1e30

A_GROUPS = ((128, 1), (512, 4), (2048, 16))
A_N_GROUPS = 3
A_HEADS = 8
A_HEAD_DIM = 128
A_OUT = A_HEADS * A_HEAD_DIM
A_QKV = 3 * A_N_GROUPS * A_HEADS * A_HEAD_DIM
BLOCK = 128

B_HEADS = 16
QK_NOPE = 128
QK_ROPE = 64
V_DIM = 128
Q_LORA = 512
KV_LORA = 512
ROPE_THETA = 10000.0

kernel_name = "yoco_dilated_mla_macaron_hybrid"


def rmsnorm(x, g):
    xf = x.astype(jnp.float32)
    y = xf * lax.rsqrt(jnp.mean(xf * xf, axis=-1, keepdims=True) + EPS)
    return (y * g.astype(jnp.float32)).astype(x.dtype)


def swiglu(x, wg, wu, wd):
    return (jax.nn.silu(x @ wg) * (x @ wu)) @ wd


def alibi_slopes(n):
    return jnp.asarray(2.0 ** (-8.0 * np.arange(1, n + 1) / n), dtype=jnp.float32)


def rope(x, pos):
    r = x.shape[-1]
    inv_freq = jnp.asarray(ROPE_THETA ** (-np.arange(0, r, 2) / r), dtype=jnp.float32)
    ang = pos.astype(jnp.float32)[..., None] * inv_freq
    ang = ang.reshape(ang.shape[:2] + (1,) * (x.ndim - 3) + (r // 2,))
    cos, sin = jnp.cos(ang), jnp.sin(ang)
    xf = x.astype(jnp.float32)
    x1, x2 = xf[..., : r // 2], xf[..., r // 2:]
    return jnp.concatenate([x1 * cos - x2 * sin, x2 * cos + x1 * sin], axis=-1).astype(x.dtype)


def dilated_window_attention(q, k, v, pos, slopes, window, dilation):
    B, S, H, Dh = q.shape
    L = S // dilation
    n_blk = -(-L // BLOCK)
    Lp = n_blk * BLOCK
    N = B * dilation
    sub_w = window // dilation

    def to_strided(t):
        t = t.reshape((B, L, dilation) + t.shape[2:])
        t = jnp.swapaxes(t, 1, 2).reshape((N, L) + t.shape[3:])
        return jnp.pad(t, [(0, 0), (0, Lp - L)] + [(0, 0)] * (t.ndim - 2))

    def key_blocks(t):
        cur = t.reshape((N, n_blk, BLOCK) + t.shape[2:])
        prev = jnp.pad(cur, [(0, 0), (1, 0)] + [(0, 0)] * (cur.ndim - 2))[:, :-1]
        return jnp.concatenate([prev, cur], axis=2)

    qs = to_strided(q).reshape(N, n_blk, BLOCK, H, Dh)
    ks = key_blocks(to_strided(k))
    vs = key_blocks(to_strided(v))
    ps = to_strided(pos)
    pq = ps.reshape(N, n_blk, BLOCK)
    pk = key_blocks(ps)

    s = jnp.einsum('nbqhd,nbkhd->nbhqk', qs, ks).astype(jnp.float32) * (Dh ** -0.5)
    dist = jnp.abs(pq[..., :, None] - pk[..., None, :]).astype(jnp.float32)
    s = s - slopes[:, None, None] * dist[:, :, None]
    qi = jnp.arange(BLOCK)[:, None]
    kj = jnp.arange(2 * BLOCK)[None, :]
    diff = BLOCK + qi - kj
    band = (diff >= 0) & (diff <= sub_w)
    exists = (jnp.arange(n_blk)[:, None, None] > 0) | (kj[None] >= BLOCK)
    mask = band[None] & exists
    s = jnp.where(mask[None, :, None], s, NEG)
    lse = jax.nn.logsumexp(s, axis=-1)
    prob = jnp.exp(s - lse[..., None])
    o = jnp.einsum('nbhqk,nbkhd->nbqhd', prob.astype(v.dtype), vs)

    o = o.reshape(N, Lp, H, Dh)[:, :L]
    o = jnp.swapaxes(o.reshape(B, dilation, L, H, Dh), 1, 2).reshape(B, S, H, Dh)
    lse = jnp.swapaxes(lse, 2, 3).reshape(N, Lp, H)[:, :L]
    lse = jnp.swapaxes(lse.reshape(B, dilation, L, H), 1, 2).reshape(B, S, H)
    return o, lse


def dilated_mixture_attention(hn, pos, w_qkv, w_o):
    B, S, _ = hn.shape
    qkv = (hn @ w_qkv).reshape(B, S, 3, A_N_GROUPS, A_HEADS, A_HEAD_DIM)
    slopes = alibi_slopes(A_N_GROUPS * A_HEADS).reshape(A_N_GROUPS, A_HEADS)
    outs, lses = [], []
    for g, (window, dilation) in enumerate(A_GROUPS):
        o, l = dilated_window_attention(qkv[:, :, 0, g], qkv[:, :, 1, g], qkv[:, :, 2, g],
                                        pos, slopes[g], window, dilation)
        outs.append(o)
        lses.append(l)
    wts = jax.nn.softmax(jnp.stack(lses, axis=0), axis=0)
    o = jnp.sum(wts[..., None] * jnp.stack(outs, axis=0).astype(jnp.float32), axis=0)
    return o.astype(hn.dtype).reshape(B, S, A_OUT) @ w_o


def shared_latent_kv(h, pos, kv_in_norm, w_dkv, kv_norm, w_ukv):
    B, S, _ = h.shape
    ckr = rmsnorm(h, kv_in_norm) @ w_dkv
    c_kv = rmsnorm(ckr[..., :KV_LORA], kv_norm)
    k_rope = rope(ckr[..., KV_LORA:], pos)
    kv = (c_kv @ w_ukv).reshape(B, S, B_HEADS, QK_NOPE + V_DIM)
    return (kv[..., :QK_NOPE], k_rope, kv[..., QK_NOPE:])


def latent_attention(hn, pos, shared, w_dq, q_norm, w_uq, w_o):
    k_nope, k_rope, v = shared
    B, S, _ = hn.shape
    c_q = rmsnorm(hn @ w_dq, q_norm)
    q = (c_q @ w_uq).reshape(B, S, B_HEADS, QK_NOPE + QK_ROPE)
    q_nope, q_rope = q[..., :QK_NOPE], rope(q[..., QK_NOPE:], pos)
    scale = (QK_NOPE + QK_ROPE) ** -0.5
    nb = S // BLOCK
    qn_b = jnp.moveaxis(q_nope.reshape(B, nb, BLOCK, B_HEADS, QK_NOPE), 1, 0)
    qr_b = jnp.moveaxis(q_rope.reshape(B, nb, BLOCK, B_HEADS, QK_ROPE), 1, 0)
    key_idx = jnp.arange(S)

    def one_block(args):
        qn, qr, blk = args
        s = (jnp.einsum('bqhd,bkhd->bhqk', qn, k_nope)
             + jnp.einsum('bqhr,bkr->bhqk', qr, k_rope)).astype(jnp.float32) * scale
        qi = blk * BLOCK + jnp.arange(BLOCK)
        s = jnp.where(key_idx[None, :] <= qi[:, None], s, NEG)
        prob = jax.nn.softmax(s, axis=-1)
        return jnp.einsum('bhqk,bkhd->bqhd', prob.astype(v.dtype), v)

    o = lax.map(one_block, (qn_b, qr_b, jnp.arange(nb)))
    o = jnp.moveaxis(o, 0, 1).reshape(B, S, B_HEADS * V_DIM)
    return o @ w_o


def setup_inputs(seed: int = 0) -> dict:
    key = jax.random.key(seed)
    ks = jax.random.split(key, 24)
    f32 = jnp.float32

    def w(k, shape, fan_in):
        return jax.random.normal(k, shape, f32) * (fan_in ** -0.5)

    def gain(k, shape):
        return 1.0 + 0.02 * jax.random.normal(k, shape, f32)

    x = jax.random.normal(ks[0], (BATCH, SEQ, D_MODEL), f32)
    p = jax.random.normal(ks[1], (DEPTH, BATCH, SEQ, PLE_DIM), f32)
    offset = jax.random.randint(ks[2], (BATCH, 1), 0, 1024, dtype=jnp.int32)
    positions = (jnp.arange(SEQ, dtype=jnp.int32)[None, :] + offset).astype(jnp.int32)
    return {
        "x": x,
        "p": p,
        "positions": positions,
        "norms": gain(ks[3], (DEPTH, N_NORMS, D_MODEL)),
        "ffn1_wg": w(ks[4], (DEPTH, D_MODEL, D_FF), D_MODEL),
        "ffn1_wu": w(ks[5], (DEPTH, D_MODEL, D_FF), D_MODEL),
        "ffn1_wd": w(ks[6], (DEPTH, D_FF, D_MODEL), D_FF),
        "ffn2_wg": w(ks[7], (DEPTH, D_MODEL, D_FF), D_MODEL),
        "ffn2_wu": w(ks[8], (DEPTH, D_MODEL, D_FF), D_MODEL),
        "ffn2_wd": w(ks[9], (DEPTH, D_FF, D_MODEL), D_FF),
        "ple_proj": w(ks[10], (DEPTH, PLE_DIM, D_MODEL), PLE_DIM),
        "ple_gate": w(ks[11], (DEPTH, D_MODEL, D_MODEL), D_MODEL),
        "a_wqkv": w(ks[12], (N_A_LAYERS, D_MODEL, A_QKV), D_MODEL),
        "a_wo": w(ks[13], (N_A_LAYERS, A_OUT, D_MODEL), A_OUT),
        "b_wdq": w(ks[14], (N_B_LAYERS, D_MODEL, Q_LORA), D_MODEL),
        "b_q_norm": gain(ks[15], (N_B_LAYERS, Q_LORA)),
        "b_wuq": w(ks[16], (N_B_LAYERS, Q_LORA, B_HEADS * (QK_NOPE + QK_ROPE)), Q_LORA),
        "b_wo": w(ks[17], (N_B_LAYERS, B_HEADS * V_DIM, D_MODEL), B_HEADS * V_DIM),
        "kv_in_norm": gain(ks[18], (D_MODEL,)),
        "w_dkv": w(ks[19], (D_MODEL, KV_LORA + QK_ROPE), D_MODEL),
        "kv_norm": gain(ks[20], (KV_LORA,)),
        "w_ukv": w(ks[21], (KV_LORA, B_HEADS * (QK_NOPE + V_DIM)), KV_LORA),
    }


def reference(x, p, positions, norms, ffn1_wg, ffn1_wu, ffn1_wd, ffn2_wg, ffn2_wu, ffn2_wd,
              ple_proj, ple_gate, a_wqkv, a_wo, b_wdq, b_q_norm, b_wuq, b_wo,
              kv_in_norm, w_dkv, kv_norm, w_ukv):
    h = x
    shared = None
    for i in range(DEPTH):
        g = norms[i]
        h = h + 0.5 * rmsnorm(swiglu(rmsnorm(h, g[0]), ffn1_wg[i], ffn1_wu[i], ffn1_wd[i]), g[1])
        hn = rmsnorm(h, g[2])
        if i < N_A_LAYERS:
            m = dilated_mixture_attention(hn, positions, a_wqkv[i], a_wo[i])
        else:
            j = i - N_A_LAYERS
            m = latent_attention(hn, positions, shared, b_wdq[j], b_q_norm[j], b_wuq[j], b_wo[j])
        h = h + rmsnorm(m, g[3])
        h = h + 0.5 * rmsnorm(swiglu(rmsnorm(h, g[4]), ffn2_wg[i], ffn2_wu[i], ffn2_wd[i]), g[5])
        gate = jax.nn.sigmoid(rmsnorm(h, g[6]) @ ple_gate[i])
        h = h + rmsnorm((p[i] @ ple_proj[i]) * gate, g[7])
        if i == N_A_LAYERS - 1:
            shared = shared_latent_kv(h, positions, kv_in_norm, w_dkv, kv_norm, w_ukv)
    return h
```

```python
import functools

import numpy as np
import jax
import jax.numpy as jnp
from jax import lax
from jax.experimental import pallas as pl
from jax.experimental.pallas import tpu as pltpu

F32 = jnp.float32
BF16 = jnp.bfloat16

EPS = 1e-6
NEG = -1e30
N_NORMS = 8

A_GROUPS = ((128, 1), (512, 4), (2048, 16))
A_N_GROUPS = 3
A_HEADS = 8
A_HEAD_DIM = 128
A_OUT = A_HEADS * A_HEAD_DIM
BLOCK = 128

B_HEADS = 16
QK_NOPE = 128
QK_ROPE = 64
V_DIM = 128
Q_LORA = 512
KV_LORA = 512
ROPE_THETA = 10000.0
QK_PAD = 256

V7X_VMEM_BYTES = 64 * 1024 * 1024
VMEM_LIMIT_BYTES = V7X_VMEM_BYTES - 8 * 1024 * 1024


def _tile(n, pref):
    if n <= pref:
        return n
    t = pref - pref % 8
    while t >= 8:
        if n % t == 0:
            return t
        t -= 8
    return n


def _params(*sem):
    return pltpu.CompilerParams(dimension_semantics=sem, vmem_limit_bytes=VMEM_LIMIT_BYTES)


def _rmsnorm(x, g):
    return x * lax.rsqrt(jnp.mean(x * x, axis=-1, keepdims=True) + EPS) * g


def _sigmoid(x):
    return 1.0 / (1.0 + jnp.exp(-x))


def _dot(a, b):
    return jnp.dot(a, b, preferred_element_type=F32)


def _dot_nt(a, b):
    return lax.dot_general(a, b, (((1,), (1,)), ((), ())), preferred_element_type=F32)


def _ffn_body(h_ref, gpre_ref, gpost_ref, wg_ref, wu_ref, wd_ref, o_ref, xn_ref):
    j = pl.program_id(1)

    @pl.when(j == 0)
    def _():
        xn_ref[...] = _rmsnorm(h_ref[...], gpre_ref[...]).astype(BF16)
        o_ref[...] = jnp.zeros_like(o_ref)

    xn = xn_ref[...]
    gate = _dot(xn, wg_ref[...])
    up = _dot(xn, wu_ref[...])
    act = (gate * _sigmoid(gate) * up).astype(BF16)
    o_ref[...] += _dot(act, wd_ref[...])

    @pl.when(j == pl.num_programs(1) - 1)
    def _():
        o_ref[...] = h_ref[...] + 0.5 * _rmsnorm(o_ref[...], gpost_ref[...])


def _ffn(h, gpre, gpost, wg, wu, wd):
    t, d = h.shape
    f = wg.shape[1]
    tm = _tile(t, 1024)
    tf = _tile(f, 512)
    return pl.pallas_call(
        _ffn_body,
        grid=(t // tm, f // tf),
        in_specs=[
            pl.BlockSpec((tm, d), lambda i, j: (i, 0), pipeline_mode=pl.Buffered(1)),
            pl.BlockSpec((1, d), lambda i, j: (0, 0)),
            pl.BlockSpec((1, d), lambda i, j: (0, 0)),
            pl.BlockSpec((d, tf), lambda i, j: (0, j)),
            pl.BlockSpec((d, tf), lambda i, j: (0, j)),
            pl.BlockSpec((tf, d), lambda i, j: (j, 0)),
        ],
        out_specs=pl.BlockSpec((tm, d), lambda i, j: (i, 0)),
        out_shape=jax.ShapeDtypeStruct((t, d), F32),
        scratch_shapes=[pltpu.VMEM((tm, d), BF16)],
        compiler_params=_params("parallel", "arbitrary"),
        name="ffn",
    )(h, gpre, gpost, wg, wu, wd)


def _norm_proj_body(h_ref, g_ref, w_ref, o_ref, xn_ref):
    @pl.when(pl.program_id(1) == 0)
    def _():
        xn_ref[...] = _rmsnorm(h_ref[...], g_ref[...]).astype(BF16)

    o_ref[...] = _dot(xn_ref[...], w_ref[...]).astype(o_ref.dtype)


def _norm_proj(h, g, w):
    t, d = h.shape
    n = w.shape[1]
    tm = _tile(t, 1024)
    tn = _tile(n, 1024)
    return pl.pallas_call(
        _norm_proj_body,
        grid=(t // tm, n // tn),
        in_specs=[
            pl.BlockSpec((tm, d), lambda i, j: (i, 0)),
            pl.BlockSpec((1, d), lambda i, j: (0, 0)),
            pl.BlockSpec((d, tn), lambda i, j: (0, j)),
        ],
        out_specs=pl.BlockSpec((tm, tn), lambda i, j: (i, j)),
        out_shape=jax.ShapeDtypeStruct((t, n), BF16),
        scratch_shapes=[pltpu.VMEM((tm, d), BF16)],
        compiler_params=_params("parallel", "arbitrary"),
        name="norm_proj",
    )(h, g, w)


def _dilated_body(q_ref, kc_ref, kp_ref, vc_ref, vp_ref, pc_ref, pp_ref, o_ref, lse_ref,
                  *, slopes, scale, sub_w):
    blk = pl.program_id(2)
    q = q_ref[0]
    k = jnp.concatenate([kp_ref[0], kc_ref[0]], axis=0)
    v = jnp.concatenate([vp_ref[0], vc_ref[0]], axis=0)
    pq = pc_ref[0].astype(F32)
    pk = jnp.concatenate([pp_ref[0], pc_ref[0]], axis=1).astype(F32)
    pq_col = jnp.transpose(jnp.broadcast_to(pq, (BLOCK, BLOCK)))
    dist = jnp.abs(jnp.concatenate([pq_col, pq_col], axis=1) - pk)

    qi = lax.broadcasted_iota(jnp.int32, (BLOCK, 2 * BLOCK), 0)
    kj = lax.broadcasted_iota(jnp.int32, (BLOCK, 2 * BLOCK), 1)
    diff = BLOCK + qi - kj
    first_key = jnp.where(blk > 0, 0, BLOCK)
    mask = (diff >= 0) & (diff <= sub_w) & (kj >= first_key)

    for h in range(A_HEADS):
        cols = slice(h * A_HEAD_DIM, (h + 1) * A_HEAD_DIM)
        s = _dot_nt(q[:, cols], k[:, cols]) * scale - slopes[h] * dist
        s = jnp.where(mask, s, NEG)
        m = jnp.max(s, axis=-1, keepdims=True)
        p = jnp.exp(s - m)
        l = jnp.sum(p, axis=-1, keepdims=True)
        o_ref[0, :, cols] = _dot(p.astype(BF16), v[:, cols]) / l
        lse_ref[0, :, h:h + 1] = m + jnp.log(l)


def _dilated_attention(qkv, pos, g, b, s):
    window, d = A_GROUPS[g]
    l = s // d
    n_blk = l // BLOCK
    width = A_OUT
    n_col = qkv.shape[1] // width
    qkv_s = qkv.reshape(b, l, d * qkv.shape[1])
    pos_s = pos.reshape(b, l, d).transpose(0, 2, 1).reshape(b * d, 1, l)
    slopes = tuple(float(2.0 ** (-8.0 * (g * A_HEADS + h + 1) / (A_N_GROUPS * A_HEADS)))
                   for h in range(A_HEADS))

    def col(c):
        return lambda bi, r, blk: (bi, blk, r * n_col + c * A_N_GROUPS + g)

    def col_prev(c):
        return lambda bi, r, blk: (bi, jnp.maximum(blk - 1, 0), r * n_col + c * A_N_GROUPS + g)

    blk_spec = (1, BLOCK, width)
    o, lse = pl.pallas_call(
        functools.partial(_dilated_body, slopes=slopes, scale=A_HEAD_DIM ** -0.5,
                          sub_w=window // d),
        grid=(b, d, n_blk),
        in_specs=[
            pl.BlockSpec(blk_spec, col(0)),
            pl.BlockSpec(blk_spec, col(1)),
            pl.BlockSpec(blk_spec, col_prev(1)),
            pl.BlockSpec(blk_spec, col(2)),
            pl.BlockSpec(blk_spec, col_prev(2)),
            pl.BlockSpec((1, 1, BLOCK), lambda bi, r, blk: (bi * d + r, 0, blk)),
            pl.BlockSpec((1, 1, BLOCK), lambda bi, r, blk: (bi * d + r, 0, jnp.maximum(blk - 1, 0))),
        ],
        out_specs=[
            pl.BlockSpec(blk_spec, lambda bi, r, blk: (bi, blk, r)),
            pl.BlockSpec((1, BLOCK, A_HEADS), lambda bi, r, blk: (bi * d + r, blk, 0)),
        ],
        out_shape=[
            jax.ShapeDtypeStruct((b, l, d * width), F32),
            jax.ShapeDtypeStruct((b * d, l, A_HEADS), F32),
        ],
        compiler_params=_params("parallel", "parallel", "arbitrary"),
        name=f"dilated_attn_g{g}",
    )(qkv_s, qkv_s, qkv_s, qkv_s, qkv_s, pos_s, pos_s)
    o = o.reshape(b * s, width)
    lse = lse.reshape(b, d, l, A_HEADS).transpose(0, 2, 1, 3).reshape(b * s, A_HEADS)
    return o, lse


def _merge_proj_body(o0_ref, o1_ref, o2_ref, l0_ref, l1_ref, l2_ref, w_ref, h_ref, g_ref, out_ref):
    l0, l1, l2 = l0_ref[...], l1_ref[...], l2_ref[...]
    mx = jnp.maximum(jnp.maximum(l0, l1), l2)
    e0, e1, e2 = jnp.exp(l0 - mx), jnp.exp(l1 - mx), jnp.exp(l2 - mx)
    inv = 1.0 / (e0 + e1 + e2)
    w0, w1, w2 = e0 * inv, e1 * inv, e2 * inv
    parts = []
    for h in range(A_HEADS):
        cols = slice(h * A_HEAD_DIM, (h + 1) * A_HEAD_DIM)
        parts.append(w0[:, h:h + 1] * o0_ref[:, cols] + w1[:, h:h + 1] * o1_ref[:, cols]
                     + w2[:, h:h + 1] * o2_ref[:, cols])
    merged = jnp.concatenate(parts, axis=1).astype(BF16)
    out_ref[...] = h_ref[...] + _rmsnorm(_dot(merged, w_ref[...]), g_ref[...])


def _merge_proj(os, lses, w, h, g):
    t, d = h.shape
    tm = _tile(t, 512)
    row = lambda i: (i, 0)
    const = lambda i: (0, 0)
    return pl.pallas_call(
        _merge_proj_body,
        grid=(t // tm,),
        in_specs=[pl.BlockSpec((tm, A_OUT), row)] * 3 + [pl.BlockSpec((tm, A_HEADS), row)] * 3 + [
            pl.BlockSpec(w.shape, const),
            pl.BlockSpec((tm, d), row),
            pl.BlockSpec((1, d), const),
        ],
        out_specs=pl.BlockSpec((tm, d), row),
        out_shape=jax.ShapeDtypeStruct((t, d), F32),
        compiler_params=_params("parallel"),
        name="merge_proj",
    )(*os, *lses, w, h, g)


def _proj_res_body(x_ref, w_ref, h_ref, g_ref, out_ref):
    out_ref[...] = h_ref[...] + _rmsnorm(_dot(x_ref[...], w_ref[...]), g_ref[...])


def _proj_res(x, w, h, g):
    t, d = h.shape
    tm = _tile(t, 512)
    row = lambda i: (i, 0)
    const = lambda i: (0, 0)
    return pl.pallas_call(
        _proj_res_body,
        grid=(t // tm,),
        in_specs=[
            pl.BlockSpec((tm, x.shape[1]), row),
            pl.BlockSpec(w.shape, const),
            pl.BlockSpec((tm, d), row),
            pl.BlockSpec((1, d), const),
        ],
        out_specs=pl.BlockSpec((tm, d), row),
        out_shape=jax.ShapeDtypeStruct((t, d), F32),
        compiler_params=_params("parallel"),
        name="proj_res",
    )(x, w, h, g)


def _ple_body(h_ref, p_ref, gpre_ref, gpost_ref, wgate_ref, wproj_ref, out_ref):
    h = h_ref[...]
    gate = _sigmoid(_dot(_rmsnorm(h, gpre_ref[...]).astype(BF16), wgate_ref[...]))
    emb = _dot(p_ref[...].astype(BF16), wproj_ref[...])
    out_ref[...] = h + _rmsnorm(emb * gate, gpost_ref[...])


def _ple(h, p, gpre, gpost, wgate, wproj):
    t, d = h.shape
    tm = _tile(t, 512)
    row = lambda i: (i, 0)
    const = lambda i: (0, 0)
    return pl.pallas_call(
        _ple_body,
        grid=(t // tm,),
        in_specs=[
            pl.BlockSpec((tm, d), row),
            pl.BlockSpec((tm, p.shape[1]), row),
            pl.BlockSpec((1, d), const),
            pl.BlockSpec((1, d), const),
            pl.BlockSpec(wgate.shape, const),
            pl.BlockSpec(wproj.shape, const),
        ],
        out_specs=pl.BlockSpec((tm, d), row),
        out_shape=jax.ShapeDtypeStruct((t, d), F32),
        compiler_params=_params("parallel"),
        name="ple",
    )(h, p, gpre, gpost, wgate, wproj)


def _rope_freq(width):
    inv = ROPE_THETA ** (-np.arange(0, QK_ROPE, 2) / QK_ROPE)
    return jnp.asarray(np.tile(inv, width // (QK_ROPE // 2)), F32).reshape(1, width)


def _rotate_half_128(x):
    lane = lax.broadcasted_iota(jnp.int32, x.shape, 1)
    first_half = (lane % QK_ROPE) < (QK_ROPE // 2)
    return jnp.where(first_half, -pltpu.roll(x, 128 - QK_ROPE // 2, 1), pltpu.roll(x, QK_ROPE // 2, 1))


def _latent_kv_body(h_ref, pos_ref, gin_ref, wd_ref, gkv_ref, wu_ref, freq_ref, k_ref, v_ref):
    xn = _rmsnorm(h_ref[...], gin_ref[...]).astype(BF16)
    ckr = _dot(xn, wd_ref[...])
    c_kv = _rmsnorm(ckr[:, :KV_LORA], gkv_ref[...]).astype(BF16)
    kv = _dot(c_kv, wu_ref[...])

    x = ckr[:, KV_LORA:]
    ang = pos_ref[...].astype(F32) * freq_ref[...]
    half = QK_ROPE // 2
    rot = jnp.concatenate([-x[:, half:], x[:, :half]], axis=1)
    k_rope = x * jnp.cos(ang) + rot * jnp.sin(ang)
    zeros = jnp.zeros_like(k_rope)
    slot_a = jnp.concatenate([k_rope, zeros], axis=1).astype(BF16)
    slot_b = jnp.concatenate([zeros, k_rope], axis=1).astype(BF16)

    per_head = QK_NOPE + V_DIM
    for h in range(B_HEADS):
        k_ref[:, h * QK_PAD:h * QK_PAD + QK_NOPE] = kv[:, h * per_head:h * per_head + QK_NOPE].astype(BF16)
        k_ref[:, h * QK_PAD + QK_NOPE:(h + 1) * QK_PAD] = slot_a if h % 2 == 0 else slot_b
        v_ref[:, h * V_DIM:(h + 1) * V_DIM] = kv[:, h * per_head + QK_NOPE:(h + 1) * per_head].astype(BF16)


def _latent_kv(h, pos_col, gin, wd, gkv, wu):
    t, d = h.shape
    tm = _tile(t, 512)
    row = lambda i: (i, 0)
    const = lambda i: (0, 0)
    freq = _rope_freq(QK_ROPE)
    return pl.pallas_call(
        _latent_kv_body,
        grid=(t // tm,),
        in_specs=[
            pl.BlockSpec((tm, d), row),
            pl.BlockSpec((tm, 1), row),
            pl.BlockSpec((1, d), const),
            pl.BlockSpec(wd.shape, const),
            pl.BlockSpec((1, KV_LORA), const),
            pl.BlockSpec(wu.shape, const),
            pl.BlockSpec(freq.shape, const),
        ],
        out_specs=[
            pl.BlockSpec((tm, B_HEADS * QK_PAD), row),
            pl.BlockSpec((tm, B_HEADS * V_DIM), row),
        ],
        out_shape=[
            jax.ShapeDtypeStruct((t, B_HEADS * QK_PAD), BF16),
            jax.ShapeDtypeStruct((t, B_HEADS * V_DIM), BF16),
        ],
        compiler_params=_params("parallel"),
        name="latent_kv",
    )(h, pos_col, gin, wd, gkv, wu, freq)


def _latent_q_body(h_ref, pos_ref, gpre_ref, wd_ref, gq_ref, wu_ref, freq_ref, q_ref, *, scale):
    xn = _rmsnorm(h_ref[...], gpre_ref[...]).astype(BF16)
    c_q = _rmsnorm(_dot(xn, wd_ref[...]), gq_ref[...]).astype(BF16)
    q = _dot(c_q, wu_ref[...])
    ang = pos_ref[...].astype(F32) * freq_ref[...]
    cos, sin = jnp.cos(ang), jnp.sin(ang)
    n_nope = B_HEADS * QK_NOPE
    for m in range(B_HEADS // 2):
        x = q[:, n_nope + m * 128:n_nope + (m + 1) * 128]
        roped = ((x * cos + _rotate_half_128(x) * sin) * scale).astype(BF16)
        for h in (2 * m, 2 * m + 1):
            q_ref[:, h * QK_PAD:h * QK_PAD + QK_NOPE] = (q[:, h * QK_NOPE:(h + 1) * QK_NOPE] * scale).astype(BF16)
            q_ref[:, h * QK_PAD + QK_NOPE:(h + 1) * QK_PAD] = roped


def _latent_q(h, pos_col, gpre, wd, gq, wu):
    t, d = h.shape
    tm = _tile(t, 512)
    row = lambda i: (i, 0)
    const = lambda i: (0, 0)
    freq = _rope_freq(128)
    return pl.pallas_call(
        functools.partial(_latent_q_body, scale=(QK_NOPE + QK_ROPE) ** -0.5),
        grid=(t // tm,),
        in_specs=[
            pl.BlockSpec((tm, d), row),
            pl.BlockSpec((tm, 1), row),
            pl.BlockSpec((1, d), const),
            pl.BlockSpec(wd.shape, const),
            pl.BlockSpec((1, Q_LORA), const),
            pl.BlockSpec(wu.shape, const),
            pl.BlockSpec(freq.shape, const),
        ],
        out_specs=pl.BlockSpec((tm, B_HEADS * QK_PAD), row),
        out_shape=jax.ShapeDtypeStruct((t, B_HEADS * QK_PAD), BF16),
        compiler_params=_params("parallel"),
        name="latent_q",
    )(h, pos_col, gpre, wd, gq, wu, freq)


def _causal_attn_body(q_ref, k_ref, v_ref, o_ref, *, tq):
    qi = pl.program_id(2)
    q = q_ref[...]

    def chunk(ki, carry, diagonal):
        m, l, acc = carry
        start = pl.multiple_of(ki * tq, tq)
        s = _dot_nt(q, k_ref[pl.ds(start, tq), :])
        if diagonal:
            row = lax.broadcasted_iota(jnp.int32, (tq, tq), 0)
            colj = lax.broadcasted_iota(jnp.int32, (tq, tq), 1)
            s = jnp.where(colj <= row, s, NEG)
        m_new = jnp.maximum(m, jnp.max(s, axis=-1, keepdims=True))
        alpha = jnp.exp(m - m_new)
        p = jnp.exp(s - m_new)
        l = alpha * l + jnp.sum(p, axis=-1, keepdims=True)
        acc = alpha * acc + _dot(p.astype(BF16), v_ref[pl.ds(start, tq), :])
        return m_new, l, acc

    carry = (jnp.full((tq, 1), NEG, F32), jnp.zeros((tq, 1), F32), jnp.zeros((tq, V_DIM), F32))
    carry = lax.fori_loop(0, qi, lambda ki, c: chunk(ki, c, False), carry)
    _, l, acc = chunk(qi, carry, True)
    o_ref[...] = (acc / l).astype(o_ref.dtype)


def _causal_attention(q, k, v, b, s):
    t = b * s
    tq = _tile(s, 512)
    nq = s // tq
    return pl.pallas_call(
        functools.partial(_causal_attn_body, tq=tq),
        grid=(b, B_HEADS, nq),
        in_specs=[
            pl.BlockSpec((tq, QK_PAD), lambda bi, h, qi: (bi * nq + qi, h)),
            pl.BlockSpec((s, QK_PAD), lambda bi, h, qi: (bi, h)),
            pl.BlockSpec((s, V_DIM), lambda bi, h, qi: (bi, h)),
        ],
        out_specs=pl.BlockSpec((tq, V_DIM), lambda bi, h, qi: (bi * nq + qi, h)),
        out_shape=jax.ShapeDtypeStruct((t, B_HEADS * V_DIM), BF16),
        compiler_params=_params("parallel", "parallel", "arbitrary"),
        name="causal_attn",
    )(q, k, v)


def kernel(x, p, positions, norms, ffn1_wg, ffn1_wu, ffn1_wd, ffn2_wg, ffn2_wu, ffn2_wd,
           ple_proj, ple_gate, a_wqkv, a_wo, b_wdq, b_q_norm, b_wuq, b_wo,
           kv_in_norm, w_dkv, kv_norm, w_ukv):
    b, s, d = x.shape
    depth = norms.shape[0]
    n_a = depth // 2
    t = b * s
    h = x.reshape(t, d)
    pos_col = positions.reshape(t, 1)
    row = lambda a: a.reshape(1, -1)
    shared = None
    for i in range(depth):
        g = [row(norms[i, n]) for n in range(N_NORMS)]
        h = _ffn(h, g[0], g[1], ffn1_wg[i].astype(BF16), ffn1_wu[i].astype(BF16), ffn1_wd[i].astype(BF16))
        if i < n_a:
            qkv = _norm_proj(h, g[2], a_wqkv[i].astype(BF16))
            outs = [_dilated_attention(qkv, positions, grp, b, s) for grp in range(A_N_GROUPS)]
            h = _merge_proj([o for o, _ in outs], [l for _, l in outs], a_wo[i].astype(BF16), h, g[3])
        else:
            j = i - n_a
            w_uq = b_wuq[j].reshape(Q_LORA, B_HEADS, QK_NOPE + QK_ROPE)
            w_uq = jnp.concatenate([w_uq[:, :, :QK_NOPE].reshape(Q_LORA, -1),
                                    w_uq[:, :, QK_NOPE:].reshape(Q_LORA, -1)], axis=1).astype(BF16)
            q = _latent_q(h, pos_col, g[2], b_wdq[j].astype(BF16), row(b_q_norm[j]), w_uq)
            o = _causal_attention(q, shared[0], shared[1], b, s)
            h = _proj_res(o, b_wo[j].astype(BF16), h, g[3])
        h = _ffn(h, g[4], g[5], ffn2_wg[i].astype(BF16), ffn2_wu[i].astype(BF16), ffn2_wd[i].astype(BF16))
        h = _ple(h, p[i].reshape(t, -1), g[6], g[7], ple_gate[i].astype(BF16), ple_proj[i].astype(BF16))
        if i == n_a - 1:
            shared = _latent_kv(h, pos_col, row(kv_in_norm), w_dkv.astype(BF16), row(kv_norm),
                                w_ukv.astype(BF16))
    return h.reshape(b, s, d)
```

```python
import functools

import numpy as np
import jax
import jax.numpy as jnp
from jax import lax
from jax.experimental import pallas as pl
from jax.experimental.pallas import tpu as pltpu

F32 = jnp.float32
BF16 = jnp.bfloat16

EPS = 1e-6
NEG = -1e30
N_NORMS = 8

A_GROUPS = ((128, 1), (512, 4), (2048, 16))
A_N_GROUPS = 3
A_HEADS = 8
A_HEAD_DIM = 128
A_OUT = A_HEADS * A_HEAD_DIM
BLOCK = 128

B_HEADS = 16
QK_NOPE = 128
QK_ROPE = 64
V_DIM = 128
Q_LORA = 512
KV_LORA = 512
ROPE_THETA = 10000.0
QK_PAD = 256

V7X_VMEM_BYTES = 64 * 1024 * 1024
VMEM_LIMIT_BYTES = V7X_VMEM_BYTES - 8 * 1024 * 1024
LANES = 128


def _tile(n, pref):
    if n <= pref:
        return n
    t = pref - pref % 8
    while t >= 8:
        if n % t == 0:
            return t
        t -= 8
    return n


def _params(*sem):
    return pltpu.CompilerParams(dimension_semantics=sem, vmem_limit_bytes=VMEM_LIMIT_BYTES)


def _rmsnorm(x, g):
    return x * lax.rsqrt(jnp.mean(x * x, axis=-1, keepdims=True) + EPS) * g


def _sigmoid(x):
    return 1.0 / (1.0 + jnp.exp(-x))


def _dot(a, b):
    return jnp.dot(a, b, preferred_element_type=F32)


def _dot_nt(a, b):
    return lax.dot_general(a, b, (((1,), (1,)), ((), ())), preferred_element_type=F32)


def _ffn_body(h_ref, gpre_ref, gpost_ref, wg_ref, wu_ref, wd_ref, o_ref, xn_ref):
    j = pl.program_id(1)

    @pl.when(j == 0)
    def _():
        xn_ref[...] = _rmsnorm(h_ref[...], gpre_ref[...]).astype(BF16)
        o_ref[...] = jnp.zeros_like(o_ref)

    xn = xn_ref[...]
    gate = _dot(xn, wg_ref[...])
    up = _dot(xn, wu_ref[...])
    act = (gate * _sigmoid(gate) * up).astype(BF16)
    o_ref[...] += _dot(act, wd_ref[...])

    @pl.when(j == pl.num_programs(1) - 1)
    def _():
        o_ref[...] = h_ref[...] + 0.5 * _rmsnorm(o_ref[...], gpost_ref[...])


def _ffn(h, gpre, gpost, wg, wu, wd):
    t, d = h.shape
    f = wg.shape[1]
    tm = _tile(t, 1024)
    tf = _tile(f, 512)
    return pl.pallas_call(
        _ffn_body,
        grid=(t // tm, f // tf),
        in_specs=[
            pl.BlockSpec((tm, d), lambda i, j: (i, 0), pipeline_mode=pl.Buffered(1)),
            pl.BlockSpec((1, d), lambda i, j: (0, 0)),
            pl.BlockSpec((1, d), lambda i, j: (0, 0)),
            pl.BlockSpec((d, tf), lambda i, j: (0, j)),
            pl.BlockSpec((d, tf), lambda i, j: (0, j)),
            pl.BlockSpec((tf, d), lambda i, j: (j, 0)),
        ],
        out_specs=pl.BlockSpec((tm, d), lambda i, j: (i, 0)),
        out_shape=jax.ShapeDtypeStruct((t, d), F32),
        scratch_shapes=[pltpu.VMEM((tm, d), BF16)],
        compiler_params=_params("parallel", "arbitrary"),
        name="ffn",
    )(h, gpre, gpost, wg, wu, wd)


def _qkv_proj_body(h_ref, g_ref, w_ref, o0_ref, o1_ref, o2_ref, xn_ref, slab_ref):
    j = pl.program_id(1)

    @pl.when(j == 0)
    def _():
        xn_ref[...] = _rmsnorm(h_ref[...], g_ref[...]).astype(BF16)

    res = _dot(xn_ref[...], w_ref[...])
    tm = res.shape[0]
    n_slab = res.shape[1] // LANES
    grp = j // 3

    @pl.when(grp == 0)
    def _():
        o0_ref[0, 0] = res.astype(BF16)

    @pl.when(grp > 0)
    def _():
        for sl in range(n_slab):
            slab_ref[sl] = res[:, sl * LANES:(sl + 1) * LANES]

    for o_ref, which in ((o1_ref, 1), (o2_ref, 2)):
        d = A_GROUPS[which][1]

        @pl.when(grp == which)
        def _(o_ref=o_ref, d=d):
            for r in range(d):
                for sl in range(n_slab):
                    o_ref[0, r, :, sl * LANES:(sl + 1) * LANES] = (
                        slab_ref[sl, pl.ds(r, tm // d, stride=d), :].astype(BF16))


def _qkv_proj(h, g, w, b, s):
    t, dm = h.shape
    tm = _tile(s, 1024)
    n_chunk = s // tm
    n_j = 3 * A_N_GROUPS

    def out_spec(which):
        d = A_GROUPS[which][1]
        return pl.BlockSpec(
            (1, d, tm // d, A_OUT),
            lambda i, j: (i // n_chunk, 0, i % n_chunk, jnp.clip(j - 3 * which, 0, 2)))

    return pl.pallas_call(
        _qkv_proj_body,
        grid=(t // tm, n_j),
        in_specs=[
            pl.BlockSpec((tm, dm), lambda i, j: (i, 0)),
            pl.BlockSpec((1, dm), lambda i, j: (0, 0)),
            pl.BlockSpec((dm, A_OUT), lambda i, j: (0, (j % 3) * A_N_GROUPS + j // 3)),
        ],
        out_specs=[out_spec(0), out_spec(1), out_spec(2)],
        out_shape=[jax.ShapeDtypeStruct((b, d, s // d, 3 * A_OUT), BF16) for _, d in A_GROUPS],
        scratch_shapes=[pltpu.VMEM((tm, dm), BF16), pltpu.VMEM((A_OUT // LANES, tm, LANES), F32)],
        compiler_params=_params("parallel", "arbitrary"),
        name="qkv_proj",
    )(h, g, w)


def _dilated_body(q_ref, kc_ref, kp_ref, vc_ref, vp_ref, pc_ref, pp_ref, o_ref, lse_ref,
                  *, slopes, scale, sub_w):
    blk = pl.program_id(2)
    q = q_ref[0, 0]
    k = jnp.concatenate([kp_ref[0, 0], kc_ref[0, 0]], axis=0)
    v = jnp.concatenate([vp_ref[0, 0], vc_ref[0, 0]], axis=0)
    pq = pc_ref[0].astype(F32)
    pk = jnp.concatenate([pp_ref[0], pc_ref[0]], axis=1).astype(F32)
    pq_col = jnp.transpose(jnp.broadcast_to(pq, (BLOCK, BLOCK)))
    dist = jnp.abs(jnp.concatenate([pq_col, pq_col], axis=1) - pk)

    qi = lax.broadcasted_iota(jnp.int32, (BLOCK, 2 * BLOCK), 0)
    kj = lax.broadcasted_iota(jnp.int32, (BLOCK, 2 * BLOCK), 1)
    diff = BLOCK + qi - kj
    first_key = jnp.where(blk > 0, 0, BLOCK)
    mask = (diff >= 0) & (diff <= sub_w) & (kj >= first_key)

    head_lane = lax.broadcasted_iota(jnp.int32, (BLOCK, LANES), 1)
    lse_tile = jnp.zeros((BLOCK, LANES), F32)
    for h in range(A_HEADS):
        cols = slice(h * A_HEAD_DIM, (h + 1) * A_HEAD_DIM)
        s = _dot_nt(q[:, cols], k[:, cols]) * scale - slopes[h] * dist
        s = jnp.where(mask, s, NEG)
        m = jnp.max(s, axis=-1, keepdims=True)
        p = jnp.exp(s - m)
        l = jnp.sum(p, axis=-1, keepdims=True)
        o_ref[0, 0, :, cols] = _dot(p.astype(BF16), v[:, cols]) / l
        lse_tile = jnp.where(head_lane == h, m + jnp.log(l), lse_tile)
    lse_ref[0, 0] = lse_tile


def _dilated_attention(qkv_g, pos, g, b, s):
    window, d = A_GROUPS[g]
    l = s // d
    n_blk = l // BLOCK
    pos_s = pos.reshape(b, l, d).transpose(0, 2, 1).reshape(b * d, 1, l)
    slopes = tuple(float(2.0 ** (-8.0 * (g * A_HEADS + h + 1) / (A_N_GROUPS * A_HEADS)))
                   for h in range(A_HEADS))

    def cur(c):
        return lambda bi, r, blk: (bi, r, blk, c)

    def prev(c):
        return lambda bi, r, blk: (bi, r, jnp.maximum(blk - 1, 0), c)

    blk_spec = (1, 1, BLOCK, A_OUT)
    return pl.pallas_call(
        functools.partial(_dilated_body, slopes=slopes, scale=A_HEAD_DIM ** -0.5,
                          sub_w=window // d),
        grid=(b, d, n_blk),
        in_specs=[
            pl.BlockSpec(blk_spec, cur(0)),
            pl.BlockSpec(blk_spec, cur(1)),
            pl.BlockSpec(blk_spec, prev(1)),
            pl.BlockSpec(blk_spec, cur(2)),
            pl.BlockSpec(blk_spec, prev(2)),
            pl.BlockSpec((1, 1, BLOCK), lambda bi, r, blk: (bi * d + r, 0, blk)),
            pl.BlockSpec((1, 1, BLOCK), lambda bi, r, blk: (bi * d + r, 0, jnp.maximum(blk - 1, 0))),
        ],
        out_specs=[
            pl.BlockSpec(blk_spec, cur(0)),
            pl.BlockSpec((1, 1, BLOCK, LANES), cur(0)),
        ],
        out_shape=[
            jax.ShapeDtypeStruct((b, d, l, A_OUT), F32),
            jax.ShapeDtypeStruct((b, d, l, LANES), F32),
        ],
        compiler_params=_params("parallel", "parallel", "arbitrary"),
        name=f"dilated_attn_g{g}",
    )(qkv_g, qkv_g, qkv_g, qkv_g, qkv_g, pos_s, pos_s)


def _merge_proj_body(o0_ref, o1_ref, o2_ref, l0_ref, l1_ref, l2_ref, w_ref, h_ref, g_ref, out_ref,
                     os1_ref, os2_ref, ls1_ref, ls2_ref):
    tm = out_ref.shape[0]
    for o_ref, l_ref, os_ref, ls_ref, which in ((o1_ref, l1_ref, os1_ref, ls1_ref, 1),
                                                (o2_ref, l2_ref, os2_ref, ls2_ref, 2)):
        d = A_GROUPS[which][1]
        for r in range(d):
            rows = pl.ds(r, tm // d, stride=d)
            ls_ref[rows, :] = l_ref[0, r]
            for h in range(A_HEADS):
                os_ref[h, rows, :] = o_ref[0, r, :, h * A_HEAD_DIM:(h + 1) * A_HEAD_DIM]

    l0, l1, l2 = l0_ref[0, 0], ls1_ref[...], ls2_ref[...]
    mx = jnp.maximum(jnp.maximum(l0, l1), l2)
    e0, e1, e2 = jnp.exp(l0 - mx), jnp.exp(l1 - mx), jnp.exp(l2 - mx)
    inv = 1.0 / (e0 + e1 + e2)
    w0, w1, w2 = e0 * inv, e1 * inv, e2 * inv
    parts = []
    for h in range(A_HEADS):
        cols = slice(h * A_HEAD_DIM, (h + 1) * A_HEAD_DIM)
        parts.append(w0[:, h:h + 1] * o0_ref[0, 0, :, cols] + w1[:, h:h + 1] * os1_ref[h]
                     + w2[:, h:h + 1] * os2_ref[h])
    merged = jnp.concatenate(parts, axis=1).astype(BF16)
    out_ref[...] = h_ref[...] + _rmsnorm(_dot(merged, w_ref[...]), g_ref[...])


def _merge_proj(os, lses, w, h, g, b, s):
    t, dm = h.shape
    tm = _tile(s, 512)
    n_chunk = s // tm
    row = lambda i: (i, 0)
    const = lambda i: (0, 0)
    grp = lambda i: (i // n_chunk, 0, i % n_chunk, 0)
    o_specs = [pl.BlockSpec((1, d, tm // d, A_OUT), grp) for _, d in A_GROUPS]
    l_specs = [pl.BlockSpec((1, d, tm // d, LANES), grp) for _, d in A_GROUPS]
    return pl.pallas_call(
        _merge_proj_body,
        grid=(t // tm,),
        in_specs=o_specs + l_specs + [
            pl.BlockSpec(w.shape, const),
            pl.BlockSpec((tm, dm), row),
            pl.BlockSpec((1, dm), const),
        ],
        out_specs=pl.BlockSpec((tm, dm), row),
        out_shape=jax.ShapeDtypeStruct((t, dm), F32),
        scratch_shapes=[pltpu.VMEM((A_HEADS, tm, A_HEAD_DIM), F32), pltpu.VMEM((A_HEADS, tm, A_HEAD_DIM), F32),
                        pltpu.VMEM((tm, LANES), F32), pltpu.VMEM((tm, LANES), F32)],
        compiler_params=_params("parallel"),
        name="merge_proj",
    )(*os, *lses, w, h, g)


def _proj_res_body(x_ref, w_ref, h_ref, g_ref, out_ref):
    out_ref[...] = h_ref[...] + _rmsnorm(_dot(x_ref[...], w_ref[...]), g_ref[...])


def _proj_res(x, w, h, g):
    t, d = h.shape
    tm = _tile(t, 512)
    row = lambda i: (i, 0)
    const = lambda i: (0, 0)
    return pl.pallas_call(
        _proj_res_body,
        grid=(t // tm,),
        in_specs=[
            pl.BlockSpec((tm, x.shape[1]), row),
            pl.BlockSpec(w.shape, const),
            pl.BlockSpec((tm, d), row),
            pl.BlockSpec((1, d), const),
        ],
        out_specs=pl.BlockSpec((tm, d), row),
        out_shape=jax.ShapeDtypeStruct((t, d), F32),
        compiler_params=_params("parallel"),
        name="proj_res",
    )(x, w, h, g)


def _ple_body(h_ref, p_ref, gpre_ref, gpost_ref, wgate_ref, wproj_ref, out_ref):
    h = h_ref[...]
    gate = _sigmoid(_dot(_rmsnorm(h, gpre_ref[...]).astype(BF16), wgate_ref[...]))
    emb = _dot(p_ref[...].astype(BF16), wproj_ref[...])
    out_ref[...] = h + _rmsnorm(emb * gate, gpost_ref[...])


def _ple(h, p, gpre, gpost, wgate, wproj):
    t, d = h.shape
    tm = _tile(t, 512)
    row = lambda i: (i, 0)
    const = lambda i: (0, 0)
    return pl.pallas_call(
        _ple_body,
        grid=(t // tm,),
        in_specs=[
            pl.BlockSpec((tm, d), row),
            pl.BlockSpec((tm, p.shape[1]), row),
            pl.BlockSpec((1, d), const),
            pl.BlockSpec((1, d), const),
            pl.BlockSpec(wgate.shape, const),
            pl.BlockSpec(wproj.shape, const),
        ],
        out_specs=pl.BlockSpec((tm, d), row),
        out_shape=jax.ShapeDtypeStruct((t, d), F32),
        compiler_params=_params("parallel"),
        name="ple",
    )(h, p, gpre, gpost, wgate, wproj)


def _rope_freq(width):
    inv = ROPE_THETA ** (-np.arange(0, QK_ROPE, 2) / QK_ROPE)
    return jnp.asarray(np.tile(inv, width // (QK_ROPE // 2)), F32).reshape(1, width)


def _rotate_half_128(x):
    lane = lax.broadcasted_iota(jnp.int32, x.shape, 1)
    first_half = (lane % QK_ROPE) < (QK_ROPE // 2)
    return jnp.where(first_half, -pltpu.roll(x, 128 - QK_ROPE // 2, 1), pltpu.roll(x, QK_ROPE // 2, 1))


def _latent_kv_body(h_ref, pos_ref, gin_ref, wd_ref, gkv_ref, wu_ref, freq_ref, k_ref, v_ref):
    xn = _rmsnorm(h_ref[...], gin_ref[...]).astype(BF16)
    ckr = _dot(xn, wd_ref[...])
    c_kv = _rmsnorm(ckr[:, :KV_LORA], gkv_ref[...]).astype(BF16)
    kv = _dot(c_kv, wu_ref[...])

    x = ckr[:, KV_LORA:]
    ang = pos_ref[...].astype(F32) * freq_ref[...]
    half = QK_ROPE // 2
    rot = jnp.concatenate([-x[:, half:], x[:, :half]], axis=1)
    k_rope = x * jnp.cos(ang) + rot * jnp.sin(ang)
    zeros = jnp.zeros_like(k_rope)
    slot_a = jnp.concatenate([k_rope, zeros], axis=1).astype(BF16)
    slot_b = jnp.concatenate([zeros, k_rope], axis=1).astype(BF16)

    per_head = QK_NOPE + V_DIM
    for h in range(B_HEADS):
        k_ref[:, h * QK_PAD:h * QK_PAD + QK_NOPE] = kv[:, h * per_head:h * per_head + QK_NOPE].astype(BF16)
        k_ref[:, h * QK_PAD + QK_NOPE:(h + 1) * QK_PAD] = slot_a if h % 2 == 0 else slot_b
        v_ref[:, h * V_DIM:(h + 1) * V_DIM] = kv[:, h * per_head + QK_NOPE:(h + 1) * per_head].astype(BF16)


def _latent_kv(h, pos_col, gin, wd, gkv, wu):
    t, d = h.shape
    tm = _tile(t, 512)
    row = lambda i: (i, 0)
    const = lambda i: (0, 0)
    freq = _rope_freq(QK_ROPE)
    return pl.pallas_call(
        _latent_kv_body,
        grid=(t // tm,),
        in_specs=[
            pl.BlockSpec((tm, d), row),
            pl.BlockSpec((tm, 1), row),
            pl.BlockSpec((1, d), const),
            pl.BlockSpec(wd.shape, const),
            pl.BlockSpec((1, KV_LORA), const),
            pl.BlockSpec(wu.shape, const),
            pl.BlockSpec(freq.shape, const),
        ],
        out_specs=[
            pl.BlockSpec((tm, B_HEADS * QK_PAD), row),
            pl.BlockSpec((tm, B_HEADS * V_DIM), row),
        ],
        out_shape=[
            jax.ShapeDtypeStruct((t, B_HEADS * QK_PAD), BF16),
            jax.ShapeDtypeStruct((t, B_HEADS * V_DIM), BF16),
        ],
        compiler_params=_params("parallel"),
        name="latent_kv",
    )(h, pos_col, gin, wd, gkv, wu, freq)


def _latent_q_body(h_ref, pos_ref, gpre_ref, wd_ref, gq_ref, wu_ref, freq_ref, q_ref, *, scale):
    xn = _rmsnorm(h_ref[...], gpre_ref[...]).astype(BF16)
    c_q = _rmsnorm(_dot(xn, wd_ref[...]), gq_ref[...]).astype(BF16)
    q = _dot(c_q, wu_ref[...])
    ang = pos_ref[...].astype(F32) * freq_ref[...]
    cos, sin = jnp.cos(ang), jnp.sin(ang)
    n_nope = B_HEADS * QK_NOPE
    for m in range(B_HEADS // 2):
        x = q[:, n_nope + m * 128:n_nope + (m + 1) * 128]
        roped = ((x * cos + _rotate_half_128(x) * sin) * scale).astype(BF16)
        for h in (2 * m, 2 * m + 1):
            q_ref[:, h * QK_PAD:h * QK_PAD + QK_NOPE] = (q[:, h * QK_NOPE:(h + 1) * QK_NOPE] * scale).astype(BF16)
            q_ref[:, h * QK_PAD + QK_NOPE:(h + 1) * QK_PAD] = roped


def _latent_q(h, pos_col, gpre, wd, gq, wu):
    t, d = h.shape
    tm = _tile(t, 512)
    row = lambda i: (i, 0)
    const = lambda i: (0, 0)
    freq = _rope_freq(128)
    return pl.pallas_call(
        functools.partial(_latent_q_body, scale=(QK_NOPE + QK_ROPE) ** -0.5),
        grid=(t // tm,),
        in_specs=[
            pl.BlockSpec((tm, d), row),
            pl.BlockSpec((tm, 1), row),
            pl.BlockSpec((1, d), const),
            pl.BlockSpec(wd.shape, const),
            pl.BlockSpec((1, Q_LORA), const),
            pl.BlockSpec(wu.shape, const),
            pl.BlockSpec(freq.shape, const),
        ],
        out_specs=pl.BlockSpec((tm, B_HEADS * QK_PAD), row),
        out_shape=jax.ShapeDtypeStruct((t, B_HEADS * QK_PAD), BF16),
        compiler_params=_params("parallel"),
        name="latent_q",
    )(h, pos_col, gpre, wd, gq, wu, freq)


HEADS_PER_STEP = 2


def _causal_attn_body(q_ref, k_ref, v_ref, o_ref, *, tq):
    qi = pl.program_id(2)

    def chunk(ki, carries, diagonal):
        start = pl.multiple_of(ki * tq, tq)
        out = []
        for hh, (m, l, acc) in enumerate(carries):
            qc = slice(hh * QK_PAD, (hh + 1) * QK_PAD)
            vc = slice(hh * V_DIM, (hh + 1) * V_DIM)
            s = _dot_nt(q_ref[:, qc], k_ref[pl.ds(start, tq), qc])
            if diagonal:
                row = lax.broadcasted_iota(jnp.int32, (tq, tq), 0)
                colj = lax.broadcasted_iota(jnp.int32, (tq, tq), 1)
                s = jnp.where(colj <= row, s, NEG)
            m_new = jnp.maximum(m, jnp.max(s, axis=-1, keepdims=True))
            alpha = jnp.exp(m - m_new)
            p = jnp.exp(s - m_new)
            l = alpha * l + jnp.sum(p, axis=-1, keepdims=True)
            acc = alpha * acc + _dot(p.astype(BF16), v_ref[pl.ds(start, tq), vc])
            out.append((m_new, l, acc))
        return tuple(out)

    init = tuple((jnp.full((tq, 1), NEG, F32), jnp.zeros((tq, 1), F32), jnp.zeros((tq, V_DIM), F32))
                 for _ in range(HEADS_PER_STEP))
    carries = lax.fori_loop(0, qi, lambda ki, c: chunk(ki, c, False), init)
    carries = chunk(qi, carries, True)
    for hh, (_, l, acc) in enumerate(carries):
        o_ref[:, hh * V_DIM:(hh + 1) * V_DIM] = (acc / l).astype(o_ref.dtype)


def _causal_attention(q, k, v, b, s):
    t = b * s
    tq = _tile(s, 512)
    nq = s // tq
    hp = HEADS_PER_STEP
    return pl.pallas_call(
        functools.partial(_causal_attn_body, tq=tq),
        grid=(b, B_HEADS // hp, nq),
        in_specs=[
            pl.BlockSpec((tq, hp * QK_PAD), lambda bi, h, qi: (bi * nq + qi, h)),
            pl.BlockSpec((s, hp * QK_PAD), lambda bi, h, qi: (bi, h)),
            pl.BlockSpec((s, hp * V_DIM), lambda bi, h, qi: (bi, h)),
        ],
        out_specs=pl.BlockSpec((tq, hp * V_DIM), lambda bi, h, qi: (bi * nq + qi, h)),
        out_shape=jax.ShapeDtypeStruct((t, B_HEADS * V_DIM), BF16),
        compiler_params=_params("parallel", "parallel", "arbitrary"),
        name="causal_attn",
    )(q, k, v)


def kernel(x, p, positions, norms, ffn1_wg, ffn1_wu, ffn1_wd, ffn2_wg, ffn2_wu, ffn2_wd,
           ple_proj, ple_gate, a_wqkv, a_wo, b_wdq, b_q_norm, b_wuq, b_wo,
           kv_in_norm, w_dkv, kv_norm, w_ukv):
    b, s, d = x.shape
    depth = norms.shape[0]
    n_a = depth // 2
    t = b * s
    h = x.reshape(t, d)
    pos_col = positions.reshape(t, 1)
    row = lambda a: a.reshape(1, -1)
    shared = None
    for i in range(depth):
        g = [row(norms[i, n]) for n in range(N_NORMS)]
        h = _ffn(h, g[0], g[1], ffn1_wg[i].astype(BF16), ffn1_wu[i].astype(BF16), ffn1_wd[i].astype(BF16))
        if i < n_a:
            qkv = _qkv_proj(h, g[2], a_wqkv[i].astype(BF16), b, s)
            outs = [_dilated_attention(qkv[grp], positions, grp, b, s) for grp in range(A_N_GROUPS)]
            h = _merge_proj([o for o, _ in outs], [l for _, l in outs], a_wo[i].astype(BF16), h, g[3], b, s)
        else:
            j = i - n_a
            w_uq = b_wuq[j].reshape(Q_LORA, B_HEADS, QK_NOPE + QK_ROPE)
            w_uq = jnp.concatenate([w_uq[:, :, :QK_NOPE].reshape(Q_LORA, -1),
                                    w_uq[:, :, QK_NOPE:].reshape(Q_LORA, -1)], axis=1).astype(BF16)
            q = _latent_q(h, pos_col, g[2], b_wdq[j].astype(BF16), row(b_q_norm[j]), w_uq)
            o = _causal_attention(q, shared[0], shared[1], b, s)
            h = _proj_res(o, b_wo[j].astype(BF16), h, g[3])
        h = _ffn(h, g[4], g[5], ffn2_wg[i].astype(BF16), ffn2_wu[i].astype(BF16), ffn2_wd[i].astype(BF16))
        h = _ple(h, p[i].reshape(t, -1), g[6], g[7], ple_gate[i].astype(BF16), ple_proj[i].astype(BF16))
        if i == n_a - 1:
            shared = _latent_kv(h, pos_col, row(kv_in_norm), w_dkv.astype(BF16), row(kv_norm),
                                w_ukv.astype(BF16))
    return h.reshape(b, s, d)
```

```python
import functools

import numpy as np
import jax
import jax.numpy as jnp
from jax import lax
from jax.experimental import pallas as pl
from jax.experimental.pallas import tpu as pltpu

F32 = jnp.float32
BF16 = jnp.bfloat16

EPS = 1e-6
NEG = -1e30
N_NORMS = 8

A_GROUPS = ((128, 1), (512, 4), (2048, 16))
A_N_GROUPS = 3
A_HEADS = 8
A_HEAD_DIM = 128
A_OUT = A_HEADS * A_HEAD_DIM
BLOCK = 128
LOG2_E = float(np.log2(np.e))
A_Q_SCALE = A_HEAD_DIM ** -0.5 * LOG2_E

B_HEADS = 16
QK_NOPE = 128
QK_ROPE = 64
V_DIM = 128
Q_LORA = 512
KV_LORA = 512
ROPE_THETA = 10000.0
QK_PAD = 256

V7X_VMEM_BYTES = 64 * 1024 * 1024
VMEM_LIMIT_BYTES = V7X_VMEM_BYTES - 8 * 1024 * 1024
LANES = 128


def _tile(n, pref):
    if n <= pref:
        return n
    t = pref - pref % 8
    while t >= 8:
        if n % t == 0:
            return t
        t -= 8
    return n


def _params(*sem):
    return pltpu.CompilerParams(dimension_semantics=sem, vmem_limit_bytes=VMEM_LIMIT_BYTES)


def _rmsnorm(x, g):
    return x * lax.rsqrt(jnp.mean(x * x, axis=-1, keepdims=True) + EPS) * g


def _sigmoid(x):
    return 1.0 / (1.0 + jnp.exp(-x))


def _dot(a, b):
    return jnp.dot(a, b, preferred_element_type=F32)


def _dot_nt(a, b):
    return lax.dot_general(a, b, (((1,), (1,)), ((), ())), preferred_element_type=F32)


def _ffn_body(h_ref, gpre_ref, gpost_ref, wg_ref, wu_ref, wd_ref, o_ref, xn_ref):
    j = pl.program_id(1)

    @pl.when(j == 0)
    def _():
        xn_ref[...] = _rmsnorm(h_ref[...], gpre_ref[...]).astype(BF16)
        o_ref[...] = jnp.zeros_like(o_ref)

    xn = xn_ref[...]
    gate = _dot(xn, wg_ref[...])
    up = _dot(xn, wu_ref[...])
    act = (gate * _sigmoid(gate) * up).astype(BF16)
    o_ref[...] += _dot(act, wd_ref[...])

    @pl.when(j == pl.num_programs(1) - 1)
    def _():
        o_ref[...] = h_ref[...] + 0.5 * _rmsnorm(o_ref[...], gpost_ref[...])


def _ffn(h, gpre, gpost, wg, wu, wd):
    t, d = h.shape
    f = wg.shape[1]
    tm = _tile(t, 1024)
    tf = _tile(f, 512)
    return pl.pallas_call(
        _ffn_body,
        grid=(t // tm, f // tf),
        in_specs=[
            pl.BlockSpec((tm, d), lambda i, j: (i, 0), pipeline_mode=pl.Buffered(1)),
            pl.BlockSpec((1, d), lambda i, j: (0, 0)),
            pl.BlockSpec((1, d), lambda i, j: (0, 0)),
            pl.BlockSpec((d, tf), lambda i, j: (0, j)),
            pl.BlockSpec((d, tf), lambda i, j: (0, j)),
            pl.BlockSpec((tf, d), lambda i, j: (j, 0)),
        ],
        out_specs=pl.BlockSpec((tm, d), lambda i, j: (i, 0)),
        out_shape=jax.ShapeDtypeStruct((t, d), F32),
        scratch_shapes=[pltpu.VMEM((tm, d), BF16)],
        compiler_params=_params("parallel", "arbitrary"),
        name="ffn",
    )(h, gpre, gpost, wg, wu, wd)


def _qkv_proj_body(h_ref, g_ref, w_ref, o0_ref, o1_ref, o2_ref, xn_ref, slab_ref):
    j = pl.program_id(1)

    @pl.when(j == 0)
    def _():
        xn_ref[...] = _rmsnorm(h_ref[...], g_ref[...]).astype(BF16)

    res = _dot(xn_ref[...], w_ref[...])
    res = res * jnp.where(j % 3 == 0, A_Q_SCALE, 1.0)
    tm = res.shape[0]
    n_slab = res.shape[1] // LANES
    grp = j // 3

    @pl.when(grp == 0)
    def _():
        o0_ref[0, 0] = res.astype(BF16)

    @pl.when(grp > 0)
    def _():
        for sl in range(n_slab):
            slab_ref[sl] = res[:, sl * LANES:(sl + 1) * LANES]

    for o_ref, which in ((o1_ref, 1), (o2_ref, 2)):
        d = A_GROUPS[which][1]

        @pl.when(grp == which)
        def _(o_ref=o_ref, d=d):
            for r in range(d):
                for sl in range(n_slab):
                    o_ref[0, r, :, sl * LANES:(sl + 1) * LANES] = (
                        slab_ref[sl, pl.ds(r, tm // d, stride=d), :].astype(BF16))


def _qkv_proj(h, g, w, b, s):
    t, dm = h.shape
    tm = _tile(s, 1024)
    n_chunk = s // tm
    n_j = 3 * A_N_GROUPS

    def out_spec(which):
        d = A_GROUPS[which][1]
        return pl.BlockSpec(
            (1, d, tm // d, A_OUT),
            lambda i, j: (i // n_chunk, 0, i % n_chunk, jnp.clip(j - 3 * which, 0, 2)))

    return pl.pallas_call(
        _qkv_proj_body,
        grid=(t // tm, n_j),
        in_specs=[
            pl.BlockSpec((tm, dm), lambda i, j: (i, 0)),
            pl.BlockSpec((1, dm), lambda i, j: (0, 0)),
            pl.BlockSpec((dm, A_OUT), lambda i, j: (0, (j % 3) * A_N_GROUPS + j // 3)),
        ],
        out_specs=[out_spec(0), out_spec(1), out_spec(2)],
        out_shape=[jax.ShapeDtypeStruct((b, d, s // d, 3 * A_OUT), BF16) for _, d in A_GROUPS],
        scratch_shapes=[pltpu.VMEM((tm, dm), BF16), pltpu.VMEM((A_OUT // LANES, tm, LANES), F32)],
        compiler_params=_params("parallel", "arbitrary"),
        name="qkv_proj",
    )(h, g, w)


MASKED_DIST = 1e30


def _dilated_body(q_ref, kc_ref, kp_ref, vc_ref, vp_ref, pc_ref, pp_ref, o_ref, lse_ref,
                  *, slopes, sub_w):
    blk = pl.program_id(2)
    q = q_ref[0, 0]
    k = jnp.concatenate([kp_ref[0, 0], kc_ref[0, 0]], axis=0)
    v = jnp.concatenate([vp_ref[0, 0], vc_ref[0, 0]], axis=0)
    pq = pc_ref[0].astype(F32)
    pk = jnp.concatenate([pp_ref[0], pc_ref[0]], axis=1).astype(F32)
    pq_col = jnp.transpose(jnp.broadcast_to(pq, (BLOCK, BLOCK)))
    dist = jnp.abs(jnp.concatenate([pq_col, pq_col], axis=1) - pk)

    qi = lax.broadcasted_iota(jnp.int32, (BLOCK, 2 * BLOCK), 0)
    kj = lax.broadcasted_iota(jnp.int32, (BLOCK, 2 * BLOCK), 1)
    diff = BLOCK + qi - kj
    first_key = jnp.where(blk > 0, 0, BLOCK)
    mask = (diff >= 0) & (diff <= sub_w) & (kj >= first_key)
    dist = jnp.where(mask, dist, MASKED_DIST)

    head_lane = lax.broadcasted_iota(jnp.int32, (BLOCK, LANES), 1)
    lse_tile = jnp.zeros((BLOCK, LANES), F32)
    head_cols = [slice(h * A_HEAD_DIM, (h + 1) * A_HEAD_DIM) for h in range(A_HEADS)]
    scores = [_dot_nt(q[:, cols], k[:, cols]) for cols in head_cols]
    for h in range(A_HEADS):
        cols = head_cols[h]
        s = scores[h] - slopes[h] * dist
        m = jnp.max(s, axis=-1, keepdims=True)
        p = jnp.exp2(s - m)
        l = jnp.sum(p, axis=-1, keepdims=True)
        o_ref[0, 0, :, cols] = _dot(p.astype(BF16), v[:, cols]) / l
        lse_tile = jnp.where(head_lane == h, m + jnp.log2(l), lse_tile)
    lse_ref[0, 0] = lse_tile


def _dilated_attention(qkv_g, pos, g, b, s):
    window, d = A_GROUPS[g]
    l = s // d
    n_blk = l // BLOCK
    pos_s = pos.reshape(b, l, d).transpose(0, 2, 1).reshape(b * d, 1, l)
    slopes = tuple(float(2.0 ** (-8.0 * (g * A_HEADS + h + 1) / (A_N_GROUPS * A_HEADS))) * LOG2_E
                   for h in range(A_HEADS))

    def cur(c):
        return lambda bi, r, blk: (bi, r, blk, c)

    def prev(c):
        return lambda bi, r, blk: (bi, r, jnp.maximum(blk - 1, 0), c)

    blk_spec = (1, 1, BLOCK, A_OUT)
    return pl.pallas_call(
        functools.partial(_dilated_body, slopes=slopes, sub_w=window // d),
        grid=(b, d, n_blk),
        in_specs=[
            pl.BlockSpec(blk_spec, cur(0)),
            pl.BlockSpec(blk_spec, cur(1)),
            pl.BlockSpec(blk_spec, prev(1)),
            pl.BlockSpec(blk_spec, cur(2)),
            pl.BlockSpec(blk_spec, prev(2)),
            pl.BlockSpec((1, 1, BLOCK), lambda bi, r, blk: (bi * d + r, 0, blk)),
            pl.BlockSpec((1, 1, BLOCK), lambda bi, r, blk: (bi * d + r, 0, jnp.maximum(blk - 1, 0))),
        ],
        out_specs=[
            pl.BlockSpec(blk_spec, cur(0)),
            pl.BlockSpec((1, 1, BLOCK, LANES), cur(0)),
        ],
        out_shape=[
            jax.ShapeDtypeStruct((b, d, l, A_OUT), F32),
            jax.ShapeDtypeStruct((b, d, l, LANES), F32),
        ],
        compiler_params=_params("parallel", "parallel", "arbitrary"),
        name=f"dilated_attn_g{g}",
    )(qkv_g, qkv_g, qkv_g, qkv_g, qkv_g, pos_s, pos_s)


def _merge_proj_body(o0_ref, o1_ref, o2_ref, l0_ref, l1_ref, l2_ref, w_ref, h_ref, g_ref, out_ref,
                     os1_ref, os2_ref, ls1_ref, ls2_ref):
    tm = out_ref.shape[0]
    for o_ref, l_ref, os_ref, ls_ref, which in ((o1_ref, l1_ref, os1_ref, ls1_ref, 1),
                                                (o2_ref, l2_ref, os2_ref, ls2_ref, 2)):
        d = A_GROUPS[which][1]
        for r in range(d):
            rows = pl.ds(r, tm // d, stride=d)
            ls_ref[rows, :] = l_ref[0, r]
            for h in range(A_HEADS):
                os_ref[h, rows, :] = o_ref[0, r, :, h * A_HEAD_DIM:(h + 1) * A_HEAD_DIM]

    l0, l1, l2 = l0_ref[0, 0], ls1_ref[...], ls2_ref[...]
    mx = jnp.maximum(jnp.maximum(l0, l1), l2)
    e0, e1, e2 = jnp.exp2(l0 - mx), jnp.exp2(l1 - mx), jnp.exp2(l2 - mx)
    inv = 1.0 / (e0 + e1 + e2)
    w0, w1, w2 = e0 * inv, e1 * inv, e2 * inv
    parts = []
    for h in range(A_HEADS):
        cols = slice(h * A_HEAD_DIM, (h + 1) * A_HEAD_DIM)
        parts.append(w0[:, h:h + 1] * o0_ref[0, 0, :, cols] + w1[:, h:h + 1] * os1_ref[h]
                     + w2[:, h:h + 1] * os2_ref[h])
    merged = jnp.concatenate(parts, axis=1).astype(BF16)
    out_ref[...] = h_ref[...] + _rmsnorm(_dot(merged, w_ref[...]), g_ref[...])


def _merge_proj(os, lses, w, h, g, b, s):
    t, dm = h.shape
    tm = _tile(s, 512)
    n_chunk = s // tm
    row = lambda i: (i, 0)
    const = lambda i: (0, 0)
    grp = lambda i: (i // n_chunk, 0, i % n_chunk, 0)
    o_specs = [pl.BlockSpec((1, d, tm // d, A_OUT), grp) for _, d in A_GROUPS]
    l_specs = [pl.BlockSpec((1, d, tm // d, LANES), grp) for _, d in A_GROUPS]
    return pl.pallas_call(
        _merge_proj_body,
        grid=(t // tm,),
        in_specs=o_specs + l_specs + [
            pl.BlockSpec(w.shape, const),
            pl.BlockSpec((tm, dm), row),
            pl.BlockSpec((1, dm), const),
        ],
        out_specs=pl.BlockSpec((tm, dm), row),
        out_shape=jax.ShapeDtypeStruct((t, dm), F32),
        scratch_shapes=[pltpu.VMEM((A_HEADS, tm, A_HEAD_DIM), F32), pltpu.VMEM((A_HEADS, tm, A_HEAD_DIM), F32),
                        pltpu.VMEM((tm, LANES), F32), pltpu.VMEM((tm, LANES), F32)],
        compiler_params=_params("parallel"),
        name="merge_proj",
    )(*os, *lses, w, h, g)


def _proj_res_body(x_ref, w_ref, h_ref, g_ref, out_ref):
    out_ref[...] = h_ref[...] + _rmsnorm(_dot(x_ref[...], w_ref[...]), g_ref[...])


def _proj_res(x, w, h, g):
    t, d = h.shape
    tm = _tile(t, 512)
    row = lambda i: (i, 0)
    const = lambda i: (0, 0)
    return pl.pallas_call(
        _proj_res_body,
        grid=(t // tm,),
        in_specs=[
            pl.BlockSpec((tm, x.shape[1]), row),
            pl.BlockSpec(w.shape, const),
            pl.BlockSpec((tm, d), row),
            pl.BlockSpec((1, d), const),
        ],
        out_specs=pl.BlockSpec((tm, d), row),
        out_shape=jax.ShapeDtypeStruct((t, d), F32),
        compiler_params=_params("parallel"),
        name="proj_res",
    )(x, w, h, g)


def _ple_body(h_ref, p_ref, gpre_ref, gpost_ref, wgate_ref, wproj_ref, out_ref):
    h = h_ref[...]
    gate = _sigmoid(_dot(_rmsnorm(h, gpre_ref[...]).astype(BF16), wgate_ref[...]))
    emb = _dot(p_ref[...].astype(BF16), wproj_ref[...])
    out_ref[...] = h + _rmsnorm(emb * gate, gpost_ref[...])


def _ple(h, p, gpre, gpost, wgate, wproj):
    t, d = h.shape
    tm = _tile(t, 512)
    row = lambda i: (i, 0)
    const = lambda i: (0, 0)
    return pl.pallas_call(
        _ple_body,
        grid=(t // tm,),
        in_specs=[
            pl.BlockSpec((tm, d), row),
            pl.BlockSpec((tm, p.shape[1]), row),
            pl.BlockSpec((1, d), const),
            pl.BlockSpec((1, d), const),
            pl.BlockSpec(wgate.shape, const),
            pl.BlockSpec(wproj.shape, const),
        ],
        out_specs=pl.BlockSpec((tm, d), row),
        out_shape=jax.ShapeDtypeStruct((t, d), F32),
        compiler_params=_params("parallel"),
        name="ple",
    )(h, p, gpre, gpost, wgate, wproj)


def _rope_freq(repeats):
    inv = ROPE_THETA ** (-np.arange(0, QK_ROPE, 2) / QK_ROPE)
    return jnp.asarray(np.tile(inv, repeats), F32)


def _latent_kv_body(h_ref, pos_ref, gin_ref, wd_ref, gkv_ref, wu_ref, freq_ref, k_ref, vt_ref):
    xn = _rmsnorm(h_ref[...], gin_ref[...]).astype(BF16)
    ckr = _dot(xn, wd_ref[...])
    c_kv = _rmsnorm(ckr[:, :KV_LORA], gkv_ref[...]).astype(BF16)
    kv = _dot(c_kv, wu_ref[...])

    x = ckr[:, KV_LORA:]
    ang = pos_ref[...].astype(F32) * freq_ref[...]
    half = QK_ROPE // 2
    rot = jnp.concatenate([-x[:, half:], x[:, :half]], axis=1)
    k_rope = x * jnp.cos(ang) + rot * jnp.sin(ang)
    zeros = jnp.zeros_like(k_rope)
    slot_a = jnp.concatenate([k_rope, zeros], axis=1).astype(BF16)
    slot_b = jnp.concatenate([zeros, k_rope], axis=1).astype(BF16)

    per_head = QK_NOPE + V_DIM
    for h in range(B_HEADS):
        k_ref[:, h * QK_PAD:h * QK_PAD + QK_NOPE] = kv[:, h * per_head:h * per_head + QK_NOPE].astype(BF16)
        k_ref[:, h * QK_PAD + QK_NOPE:(h + 1) * QK_PAD] = slot_a if h % 2 == 0 else slot_b
        v_h = kv[:, h * per_head + QK_NOPE:(h + 1) * per_head]
        vt_ref[h * V_DIM:(h + 1) * V_DIM, :] = jnp.transpose(v_h).astype(BF16)


def _latent_kv(h, pos_col, gin, wd, gkv, wu):
    t, d = h.shape
    tm = _tile(t, 512)
    row = lambda i: (i, 0)
    const = lambda i: (0, 0)
    freq = _rope_freq(2).reshape(1, QK_ROPE)
    return pl.pallas_call(
        _latent_kv_body,
        grid=(t // tm,),
        in_specs=[
            pl.BlockSpec((tm, d), row),
            pl.BlockSpec((tm, 1), row),
            pl.BlockSpec((1, d), const),
            pl.BlockSpec(wd.shape, const),
            pl.BlockSpec((1, KV_LORA), const),
            pl.BlockSpec(wu.shape, const),
            pl.BlockSpec(freq.shape, const),
        ],
        out_specs=[
            pl.BlockSpec((tm, B_HEADS * QK_PAD), row),
            pl.BlockSpec((B_HEADS * V_DIM, tm), lambda i: (0, i)),
        ],
        out_shape=[
            jax.ShapeDtypeStruct((t, B_HEADS * QK_PAD), BF16),
            jax.ShapeDtypeStruct((B_HEADS * V_DIM, t), BF16),
        ],
        compiler_params=_params("parallel"),
        name="latent_kv",
    )(h, pos_col, gin, wd, gkv, wu, freq)


def _latent_q_body(h_ref, pos_ref, gpre_ref, wd_ref, gq_ref, wut_ref, freq_ref, qt_ref, *, scale):
    xn = _rmsnorm(h_ref[...], gpre_ref[...]).astype(BF16)
    c_q = _rmsnorm(_dot(xn, wd_ref[...]), gq_ref[...])
    qt = _dot(wut_ref[...], jnp.transpose(c_q).astype(BF16)) * scale
    ang = freq_ref[...] * pos_ref[...].astype(F32)
    cos, sin = jnp.cos(ang), jnp.sin(ang)
    n_nope = B_HEADS * QK_NOPE
    half = QK_ROPE // 2
    for m in range(B_HEADS // 2):
        pieces = []
        for h in (2 * m, 2 * m + 1):
            x1 = qt[n_nope + h * QK_ROPE:n_nope + h * QK_ROPE + half]
            x2 = qt[n_nope + h * QK_ROPE + half:n_nope + (h + 1) * QK_ROPE]
            pieces += [x1 * cos - x2 * sin, x2 * cos + x1 * sin]
        roped = jnp.concatenate(pieces, axis=0).astype(BF16)
        for h in (2 * m, 2 * m + 1):
            qt_ref[h * QK_PAD:h * QK_PAD + QK_NOPE, :] = qt[h * QK_NOPE:(h + 1) * QK_NOPE].astype(BF16)
            qt_ref[h * QK_PAD + QK_NOPE:(h + 1) * QK_PAD, :] = roped


def _latent_q(h, pos_row, gpre, wd, gq, wut):
    t, d = h.shape
    tm = _tile(t, 512)
    row = lambda i: (i, 0)
    const = lambda i: (0, 0)
    freq = _rope_freq(1).reshape(QK_ROPE // 2, 1)
    scale = (QK_NOPE + QK_ROPE) ** -0.5 * LOG2_E
    return pl.pallas_call(
        functools.partial(_latent_q_body, scale=scale),
        grid=(t // tm,),
        in_specs=[
            pl.BlockSpec((tm, d), row),
            pl.BlockSpec((1, tm), lambda i: (0, i)),
            pl.BlockSpec((1, d), const),
            pl.BlockSpec(wd.shape, const),
            pl.BlockSpec((1, Q_LORA), const),
            pl.BlockSpec(wut.shape, const),
            pl.BlockSpec(freq.shape, const),
        ],
        out_specs=pl.BlockSpec((B_HEADS * QK_PAD, tm), lambda i: (0, i)),
        out_shape=jax.ShapeDtypeStruct((B_HEADS * QK_PAD, t), BF16),
        compiler_params=_params("parallel"),
        name="latent_q",
    )(h, pos_row, gpre, wd, gq, wut, freq)


HEADS_PER_STEP = 4


def _causal_attn_body(qt_ref, k_ref, vt_ref, o_ref, *, tq):
    qi = pl.program_id(2)

    def chunk(ki, carries, diagonal):
        start = pl.multiple_of(ki * tq, tq)
        out = []
        scores = []
        for hh in range(HEADS_PER_STEP):
            kc = slice(hh * QK_PAD, (hh + 1) * QK_PAD)
            scores.append(_dot(k_ref[pl.ds(start, tq), kc], qt_ref[kc, :]))
        for hh, (m, l, acc) in enumerate(carries):
            vr = slice(hh * V_DIM, (hh + 1) * V_DIM)
            st = scores[hh]
            if diagonal:
                key = lax.broadcasted_iota(jnp.int32, (tq, tq), 0)
                qry = lax.broadcasted_iota(jnp.int32, (tq, tq), 1)
                st = jnp.where(key <= qry, st, NEG)
            m_new = jnp.maximum(m, jnp.max(st, axis=0, keepdims=True))
            alpha = jnp.exp2(m - m_new)
            p = jnp.exp2(st - m_new)
            l = alpha * l + jnp.sum(p, axis=0, keepdims=True)
            acc = alpha * acc + _dot(vt_ref[vr, pl.ds(start, tq)], p.astype(BF16))
            out.append((m_new, l, acc))
        return tuple(out)

    init = tuple((jnp.full((1, tq), NEG, F32), jnp.zeros((1, tq), F32), jnp.zeros((V_DIM, tq), F32))
                 for _ in range(HEADS_PER_STEP))
    carries = lax.fori_loop(0, qi, lambda ki, c: chunk(ki, c, False), init)
    carries = chunk(qi, carries, True)
    for hh, (_, l, acc) in enumerate(carries):
        o_ref[:, hh * V_DIM:(hh + 1) * V_DIM] = jnp.transpose(acc / l).astype(o_ref.dtype)


def _causal_attention(qt, k, vt, b, s):
    t = b * s
    tq = _tile(s, 512)
    nq = s // tq
    hp = HEADS_PER_STEP
    return pl.pallas_call(
        functools.partial(_causal_attn_body, tq=tq),
        grid=(b, B_HEADS // hp, nq),
        in_specs=[
            pl.BlockSpec((hp * QK_PAD, tq), lambda bi, h, qi: (h, bi * nq + qi)),
            pl.BlockSpec((s, hp * QK_PAD), lambda bi, h, qi: (bi, h)),
            pl.BlockSpec((hp * V_DIM, s), lambda bi, h, qi: (h, bi)),
        ],
        out_specs=pl.BlockSpec((tq, hp * V_DIM), lambda bi, h, qi: (bi * nq + qi, h)),
        out_shape=jax.ShapeDtypeStruct((t, B_HEADS * V_DIM), BF16),
        compiler_params=_params("parallel", "parallel", "arbitrary"),
        name="causal_attn",
    )(qt, k, vt)


def kernel(x, p, positions, norms, ffn1_wg, ffn1_wu, ffn1_wd, ffn2_wg, ffn2_wu, ffn2_wd,
           ple_proj, ple_gate, a_wqkv, a_wo, b_wdq, b_q_norm, b_wuq, b_wo,
           kv_in_norm, w_dkv, kv_norm, w_ukv):
    b, s, d = x.shape
    depth = norms.shape[0]
    n_a = depth // 2
    t = b * s
    h = x.reshape(t, d)
    pos_col = positions.reshape(t, 1)
    row = lambda a: a.reshape(1, -1)
    shared = None
    for i in range(depth):
        g = [row(norms[i, n]) for n in range(N_NORMS)]
        h = _ffn(h, g[0], g[1], ffn1_wg[i].astype(BF16), ffn1_wu[i].astype(BF16), ffn1_wd[i].astype(BF16))
        if i < n_a:
            qkv = _qkv_proj(h, g[2], a_wqkv[i].astype(BF16), b, s)
            outs = [_dilated_attention(qkv[grp], positions, grp, b, s) for grp in range(A_N_GROUPS)]
            h = _merge_proj([o for o, _ in outs], [l for _, l in outs], a_wo[i].astype(BF16), h, g[3], b, s)
        else:
            j = i - n_a
            w_uq = b_wuq[j].reshape(Q_LORA, B_HEADS, QK_NOPE + QK_ROPE)
            w_uqt = jnp.concatenate([w_uq[:, :, :QK_NOPE].reshape(Q_LORA, -1),
                                     w_uq[:, :, QK_NOPE:].reshape(Q_LORA, -1)], axis=1).T.astype(BF16)
            qt = _latent_q(h, positions.reshape(1, t), g[2], b_wdq[j].astype(BF16), row(b_q_norm[j]), w_uqt)
            o = _causal_attention(qt, shared[0], shared[1], b, s)
            h = _proj_res(o, b_wo[j].astype(BF16), h, g[3])
        h = _ffn(h, g[4], g[5], ffn2_wg[i].astype(BF16), ffn2_wu[i].astype(BF16), ffn2_wd[i].astype(BF16))
        h = _ple(h, p[i].reshape(t, -1), g[6], g[7], ple_gate[i].astype(BF16), ple_proj[i].astype(BF16))
        if i == n_a - 1:
            shared = _latent_kv(h, pos_col, row(kv_in_norm), w_dkv.astype(BF16), row(kv_norm),
                                w_ukv.astype(BF16))
    return h.reshape(b, s, d)
```

```python
import functools

import numpy as np
import jax
import jax.numpy as jnp
from jax import lax
from jax.experimental import pallas as pl
from jax.experimental.pallas import tpu as pltpu

F32 = jnp.float32
BF16 = jnp.bfloat16

EPS = 1e-6
NEG = -1e30
N_NORMS = 8

A_GROUPS = ((128, 1), (512, 4), (2048, 16))
A_N_GROUPS = 3
A_HEADS = 8
A_HEAD_DIM = 128
A_OUT = A_HEADS * A_HEAD_DIM
BLOCK = 128
LOG2_E = float(np.log2(np.e))
A_Q_SCALE = A_HEAD_DIM ** -0.5 * LOG2_E

B_HEADS = 16
QK_NOPE = 128
QK_ROPE = 64
V_DIM = 128
Q_LORA = 512
KV_LORA = 512
ROPE_THETA = 10000.0
QK_PAD = 256

V7X_VMEM_BYTES = 64 * 1024 * 1024
VMEM_LIMIT_BYTES = V7X_VMEM_BYTES - 2 * 1024 * 1024
LANES = 128


def _tile(n, pref):
    if n <= pref:
        return n
    t = pref - pref % 8
    while t >= 8:
        if n % t == 0:
            return t
        t -= 8
    return n


def _params(*sem):
    return pltpu.CompilerParams(dimension_semantics=sem, vmem_limit_bytes=VMEM_LIMIT_BYTES)


def _rmsnorm(x, g):
    return x * lax.rsqrt(jnp.mean(x * x, axis=-1, keepdims=True) + EPS) * g


def _sigmoid(x):
    return 1.0 / (1.0 + jnp.exp(-x))


def _dot(a, b):
    return jnp.dot(a, b, preferred_element_type=F32)


def _dot_nt(a, b):
    return lax.dot_general(a, b, (((1,), (1,)), ((), ())), preferred_element_type=F32)


ROW_CHUNK = 32


def _for_row_chunks(n_rows, fn):
    def body(c, carry):
        fn(pl.ds(pl.multiple_of(c * ROW_CHUNK, ROW_CHUNK), ROW_CHUNK))
        return carry

    lax.fori_loop(0, n_rows // ROW_CHUNK, body, 0, unroll=True)


def _ffn_body(h_ref, gpre_ref, gpost_ref, wg_ref, wu_ref, wd_ref, o_ref, xn_ref):
    j = pl.program_id(1)
    tm, d = h_ref.shape

    @pl.when(j == 0)
    def _():
        gpre = gpre_ref[...]

        def pre(rows):
            xn_ref[rows, :] = _rmsnorm(h_ref[rows, :], gpre).astype(BF16)
            o_ref[rows, :] = jnp.zeros((ROW_CHUNK, d), F32)

        _for_row_chunks(tm, pre)

    xn = xn_ref[...]
    gate = _dot(xn, wg_ref[...])
    up = _dot(xn, wu_ref[...])
    act = (gate * _sigmoid(gate) * up).astype(BF16)
    o_ref[...] += _dot(act, wd_ref[...])

    @pl.when(j == pl.num_programs(1) - 1)
    def _():
        half_gain = 0.5 * gpost_ref[...]

        def post(rows):
            o_ref[rows, :] = h_ref[rows, :] + _rmsnorm(o_ref[rows, :], half_gain)

        _for_row_chunks(tm, post)


def _ffn(h, gpre, gpost, wg, wu, wd, layer):
    t, d = h.shape
    f = wg.shape[2]
    tm = _tile(t, 1024)
    tf = _tile(f, 512)
    return pl.pallas_call(
        _ffn_body,
        grid=(t // tm, f // tf),
        in_specs=[
            pl.BlockSpec((tm, d), lambda i, j: (i, 0)),
            pl.BlockSpec((1, d), lambda i, j: (0, 0)),
            pl.BlockSpec((1, d), lambda i, j: (0, 0)),
            pl.BlockSpec((None, d, tf), lambda i, j: (layer, 0, j)),
            pl.BlockSpec((None, d, tf), lambda i, j: (layer, 0, j)),
            pl.BlockSpec((None, tf, d), lambda i, j: (layer, j, 0)),
        ],
        out_specs=pl.BlockSpec((tm, d), lambda i, j: (i, 0)),
        out_shape=jax.ShapeDtypeStruct((t, d), F32),
        scratch_shapes=[pltpu.VMEM((tm, d), BF16)],
        compiler_params=_params("parallel", "arbitrary"),
        name="ffn",
    )(h, gpre, gpost, wg, wu, wd)


def _qkv_proj_body(h_ref, g_ref, w_ref, o0_ref, o1_ref, o2_ref, xn_ref, slab_ref):
    j = pl.program_id(1)

    @pl.when(j == 0)
    def _():
        xn_ref[...] = _rmsnorm(h_ref[...], g_ref[...]).astype(BF16)

    res = _dot(xn_ref[...], w_ref[...])
    res = res * jnp.where(j % 3 == 0, A_Q_SCALE, 1.0)
    tm = res.shape[0]
    n_slab = res.shape[1] // LANES
    grp = j // 3

    @pl.when(grp == 0)
    def _():
        o0_ref[0, 0] = res.astype(BF16)

    @pl.when(grp > 0)
    def _():
        for sl in range(n_slab):
            slab_ref[sl] = res[:, sl * LANES:(sl + 1) * LANES]

    for o_ref, which in ((o1_ref, 1), (o2_ref, 2)):
        d = A_GROUPS[which][1]

        @pl.when(grp == which)
        def _(o_ref=o_ref, d=d):
            for r in range(d):
                for sl in range(n_slab):
                    o_ref[0, r, :, sl * LANES:(sl + 1) * LANES] = (
                        slab_ref[sl, pl.ds(r, tm // d, stride=d), :].astype(BF16))


def _qkv_proj(h, g, w, b, s):
    t, dm = h.shape
    tm = _tile(s, 1024)
    n_chunk = s // tm
    n_j = 3 * A_N_GROUPS

    def out_spec(which):
        d = A_GROUPS[which][1]
        return pl.BlockSpec(
            (1, d, tm // d, A_OUT),
            lambda i, j: (i // n_chunk, 0, i % n_chunk, jnp.clip(j - 3 * which, 0, 2)))

    return pl.pallas_call(
        _qkv_proj_body,
        grid=(t // tm, n_j),
        in_specs=[
            pl.BlockSpec((tm, dm), lambda i, j: (i, 0)),
            pl.BlockSpec((1, dm), lambda i, j: (0, 0)),
            pl.BlockSpec((dm, A_OUT), lambda i, j: (0, (j % 3) * A_N_GROUPS + j // 3)),
        ],
        out_specs=[out_spec(0), out_spec(1), out_spec(2)],
        out_shape=[jax.ShapeDtypeStruct((b, d, s // d, 3 * A_OUT), BF16) for _, d in A_GROUPS],
        scratch_shapes=[pltpu.VMEM((tm, dm), BF16), pltpu.VMEM((A_OUT // LANES, tm, LANES), F32)],
        compiler_params=_params("parallel", "arbitrary"),
        name="qkv_proj",
    )(h, g, w)


MASKED_DIST = 1e30


def _dilated_body(q_ref, kc_ref, kp_ref, vc_ref, vp_ref, pc_ref, pp_ref, o_ref, lse_ref,
                  *, slopes, sub_w):
    blk = pl.program_id(2)
    q = q_ref[0, 0]
    k = jnp.concatenate([kp_ref[0, 0], kc_ref[0, 0]], axis=0)
    v = jnp.concatenate([vp_ref[0, 0], vc_ref[0, 0]], axis=0)
    pq = pc_ref[0].astype(F32)
    pk = jnp.concatenate([pp_ref[0], pc_ref[0]], axis=1).astype(F32)
    pq_col = jnp.transpose(jnp.broadcast_to(pq, (BLOCK, BLOCK)))
    dist = jnp.abs(jnp.concatenate([pq_col, pq_col], axis=1) - pk)

    qi = lax.broadcasted_iota(jnp.int32, (BLOCK, 2 * BLOCK), 0)
    kj = lax.broadcasted_iota(jnp.int32, (BLOCK, 2 * BLOCK), 1)
    diff = BLOCK + qi - kj
    first_key = jnp.where(blk > 0, 0, BLOCK)
    mask = (diff >= 0) & (diff <= sub_w) & (kj >= first_key)
    dist = jnp.where(mask, dist, MASKED_DIST)

    head_lane = lax.broadcasted_iota(jnp.int32, (BLOCK, LANES), 1)
    lse_tile = jnp.zeros((BLOCK, LANES), F32)
    head_cols = [slice(h * A_HEAD_DIM, (h + 1) * A_HEAD_DIM) for h in range(A_HEADS)]
    scores = [_dot_nt(q[:, cols], k[:, cols]) for cols in head_cols]
    for h in range(A_HEADS):
        cols = head_cols[h]
        s = scores[h] - slopes[h] * dist
        m = jnp.max(s, axis=-1, keepdims=True)
        p = jnp.exp2(s - m)
        l = jnp.sum(p, axis=-1, keepdims=True)
        o_ref[0, 0, :, cols] = _dot(p.astype(BF16), v[:, cols]) / l
        lse_tile = jnp.where(head_lane == h, m + jnp.log2(l), lse_tile)
    lse_ref[0, 0] = lse_tile


def _dilated_attention(qkv_g, pos, g, b, s):
    window, d = A_GROUPS[g]
    l = s // d
    n_blk = l // BLOCK
    pos_s = pos.reshape(b, l, d).transpose(0, 2, 1).reshape(b * d, 1, l)
    slopes = tuple(float(2.0 ** (-8.0 * (g * A_HEADS + h + 1) / (A_N_GROUPS * A_HEADS))) * LOG2_E
                   for h in range(A_HEADS))

    def cur(c):
        return lambda bi, r, blk: (bi, r, blk, c)

    def prev(c):
        return lambda bi, r, blk: (bi, r, jnp.maximum(blk - 1, 0), c)

    blk_spec = (1, 1, BLOCK, A_OUT)
    return pl.pallas_call(
        functools.partial(_dilated_body, slopes=slopes, sub_w=window // d),
        grid=(b, d, n_blk),
        in_specs=[
            pl.BlockSpec(blk_spec, cur(0)),
            pl.BlockSpec(blk_spec, cur(1)),
            pl.BlockSpec(blk_spec, prev(1)),
            pl.BlockSpec(blk_spec, cur(2)),
            pl.BlockSpec(blk_spec, prev(2)),
            pl.BlockSpec((1, 1, BLOCK), lambda bi, r, blk: (bi * d + r, 0, blk)),
            pl.BlockSpec((1, 1, BLOCK), lambda bi, r, blk: (bi * d + r, 0, jnp.maximum(blk - 1, 0))),
        ],
        out_specs=[
            pl.BlockSpec(blk_spec, cur(0)),
            pl.BlockSpec((1, 1, BLOCK, LANES), cur(0)),
        ],
        out_shape=[
            jax.ShapeDtypeStruct((b, d, l, A_OUT), F32),
            jax.ShapeDtypeStruct((b, d, l, LANES), F32),
        ],
        compiler_params=_params("parallel", "parallel", "arbitrary"),
        name=f"dilated_attn_g{g}",
    )(qkv_g, qkv_g, qkv_g, qkv_g, qkv_g, pos_s, pos_s)


def _merge_proj_body(o0_ref, o1_ref, o2_ref, l0_ref, l1_ref, l2_ref, w_ref, h_ref, g_ref, out_ref,
                     os1_ref, os2_ref, ls1_ref, ls2_ref):
    tm = out_ref.shape[0]
    for o_ref, l_ref, os_ref, ls_ref, which in ((o1_ref, l1_ref, os1_ref, ls1_ref, 1),
                                                (o2_ref, l2_ref, os2_ref, ls2_ref, 2)):
        d = A_GROUPS[which][1]
        for r in range(d):
            rows = pl.ds(r, tm // d, stride=d)
            ls_ref[rows, :] = l_ref[0, r]
            for h in range(A_HEADS):
                os_ref[h, rows, :] = o_ref[0, r, :, h * A_HEAD_DIM:(h + 1) * A_HEAD_DIM]

    l0, l1, l2 = l0_ref[0, 0], ls1_ref[...], ls2_ref[...]
    mx = jnp.maximum(jnp.maximum(l0, l1), l2)
    e0, e1, e2 = jnp.exp2(l0 - mx), jnp.exp2(l1 - mx), jnp.exp2(l2 - mx)
    inv = 1.0 / (e0 + e1 + e2)
    w0, w1, w2 = e0 * inv, e1 * inv, e2 * inv
    parts = []
    for h in range(A_HEADS):
        cols = slice(h * A_HEAD_DIM, (h + 1) * A_HEAD_DIM)
        parts.append(w0[:, h:h + 1] * o0_ref[0, 0, :, cols] + w1[:, h:h + 1] * os1_ref[h]
                     + w2[:, h:h + 1] * os2_ref[h])
    merged = jnp.concatenate(parts, axis=1).astype(BF16)
    out_ref[...] = h_ref[...] + _rmsnorm(_dot(merged, w_ref[...]), g_ref[...])


def _merge_proj(os, lses, w, h, g, b, s):
    t, dm = h.shape
    tm = _tile(s, 512)
    n_chunk = s // tm
    row = lambda i: (i, 0)
    const = lambda i: (0, 0)
    grp = lambda i: (i // n_chunk, 0, i % n_chunk, 0)
    o_specs = [pl.BlockSpec((1, d, tm // d, A_OUT), grp) for _, d in A_GROUPS]
    l_specs = [pl.BlockSpec((1, d, tm // d, LANES), grp) for _, d in A_GROUPS]
    return pl.pallas_call(
        _merge_proj_body,
        grid=(t // tm,),
        in_specs=o_specs + l_specs + [
            pl.BlockSpec(w.shape, const),
            pl.BlockSpec((tm, dm), row),
            pl.BlockSpec((1, dm), const),
        ],
        out_specs=pl.BlockSpec((tm, dm), row),
        out_shape=jax.ShapeDtypeStruct((t, dm), F32),
        scratch_shapes=[pltpu.VMEM((A_HEADS, tm, A_HEAD_DIM), F32), pltpu.VMEM((A_HEADS, tm, A_HEAD_DIM), F32),
                        pltpu.VMEM((tm, LANES), F32), pltpu.VMEM((tm, LANES), F32)],
        compiler_params=_params("parallel"),
        name="merge_proj",
    )(*os, *lses, w, h, g)


def _proj_res_body(x_ref, w_ref, h_ref, g_ref, out_ref):
    out_ref[...] = h_ref[...] + _rmsnorm(_dot(x_ref[...], w_ref[...]), g_ref[...])


def _proj_res(x, w, h, g):
    t, d = h.shape
    tm = _tile(t, 512)
    row = lambda i: (i, 0)
    const = lambda i: (0, 0)
    return pl.pallas_call(
        _proj_res_body,
        grid=(t // tm,),
        in_specs=[
            pl.BlockSpec((tm, x.shape[1]), row),
            pl.BlockSpec(w.shape, const),
            pl.BlockSpec((tm, d), row),
            pl.BlockSpec((1, d), const),
        ],
        out_specs=pl.BlockSpec((tm, d), row),
        out_shape=jax.ShapeDtypeStruct((t, d), F32),
        compiler_params=_params("parallel"),
        name="proj_res",
    )(x, w, h, g)


def _ple_body(h_ref, p_ref, gpre_ref, gpost_ref, wgate_ref, wproj_ref, out_ref):
    h = h_ref[...]
    gate = _sigmoid(_dot(_rmsnorm(h, gpre_ref[...]).astype(BF16), wgate_ref[...]))
    emb = _dot(p_ref[...].astype(BF16), wproj_ref[...])
    out_ref[...] = h + _rmsnorm(emb * gate, gpost_ref[...])


def _ple(h, p, gpre, gpost, wgate, wproj):
    t, d = h.shape
    tm = _tile(t, 512)
    row = lambda i: (i, 0)
    const = lambda i: (0, 0)
    return pl.pallas_call(
        _ple_body,
        grid=(t // tm,),
        in_specs=[
            pl.BlockSpec((tm, d), row),
            pl.BlockSpec((tm, p.shape[1]), row),
            pl.BlockSpec((1, d), const),
            pl.BlockSpec((1, d), const),
            pl.BlockSpec(wgate.shape, const),
            pl.BlockSpec(wproj.shape, const),
        ],
        out_specs=pl.BlockSpec((tm, d), row),
        out_shape=jax.ShapeDtypeStruct((t, d), F32),
        compiler_params=_params("parallel"),
        name="ple",
    )(h, p, gpre, gpost, wgate, wproj)


def _rope_freq(repeats):
    inv = ROPE_THETA ** (-np.arange(0, QK_ROPE, 2) / QK_ROPE)
    return jnp.asarray(np.tile(inv, repeats), F32)


def _latent_kv_body(h_ref, pos_ref, gin_ref, wd_ref, gkv_ref, wu_ref, freq_ref, k_ref, vt_ref):
    xn = _rmsnorm(h_ref[...], gin_ref[...]).astype(BF16)
    ckr = _dot(xn, wd_ref[...])
    c_kv = _rmsnorm(ckr[:, :KV_LORA], gkv_ref[...]).astype(BF16)
    kv = _dot(c_kv, wu_ref[...])

    x = ckr[:, KV_LORA:]
    ang = pos_ref[...].astype(F32) * freq_ref[...]
    half = QK_ROPE // 2
    rot = jnp.concatenate([-x[:, half:], x[:, :half]], axis=1)
    k_rope = x * jnp.cos(ang) + rot * jnp.sin(ang)
    zeros = jnp.zeros_like(k_rope)
    slot_a = jnp.concatenate([k_rope, zeros], axis=1).astype(BF16)
    slot_b = jnp.concatenate([zeros, k_rope], axis=1).astype(BF16)

    per_head = QK_NOPE + V_DIM
    for h in range(B_HEADS):
        k_ref[:, h * QK_PAD:h * QK_PAD + QK_NOPE] = kv[:, h * per_head:h * per_head + QK_NOPE].astype(BF16)
        k_ref[:, h * QK_PAD + QK_NOPE:(h + 1) * QK_PAD] = slot_a if h % 2 == 0 else slot_b
        v_h = kv[:, h * per_head + QK_NOPE:(h + 1) * per_head]
        vt_ref[h * V_DIM:(h + 1) * V_DIM, :] = jnp.transpose(v_h).astype(BF16)


def _latent_kv(h, pos_col, gin, wd, gkv, wu):
    t, d = h.shape
    tm = _tile(t, 512)
    row = lambda i: (i, 0)
    const = lambda i: (0, 0)
    freq = _rope_freq(2).reshape(1, QK_ROPE)
    return pl.pallas_call(
        _latent_kv_body,
        grid=(t // tm,),
        in_specs=[
            pl.BlockSpec((tm, d), row),
            pl.BlockSpec((tm, 1), row),
            pl.BlockSpec((1, d), const),
            pl.BlockSpec(wd.shape, const),
            pl.BlockSpec((1, KV_LORA), const),
            pl.BlockSpec(wu.shape, const),
            pl.BlockSpec(freq.shape, const),
        ],
        out_specs=[
            pl.BlockSpec((tm, B_HEADS * QK_PAD), row),
            pl.BlockSpec((B_HEADS * V_DIM, tm), lambda i: (0, i)),
        ],
        out_shape=[
            jax.ShapeDtypeStruct((t, B_HEADS * QK_PAD), BF16),
            jax.ShapeDtypeStruct((B_HEADS * V_DIM, t), BF16),
        ],
        compiler_params=_params("parallel"),
        name="latent_kv",
    )(h, pos_col, gin, wd, gkv, wu, freq)


def _latent_q_body(h_ref, pos_ref, gpre_ref, wd_ref, gq_ref, wut_ref, freq_ref, qt_ref, *, scale):
    xn = _rmsnorm(h_ref[...], gpre_ref[...]).astype(BF16)
    c_q = _rmsnorm(_dot(xn, wd_ref[...]), gq_ref[...])
    qt = _dot(wut_ref[...], jnp.transpose(c_q).astype(BF16)) * scale
    ang = freq_ref[...] * pos_ref[...].astype(F32)
    cos, sin = jnp.cos(ang), jnp.sin(ang)
    n_nope = B_HEADS * QK_NOPE
    half = QK_ROPE // 2
    for m in range(B_HEADS // 2):
        pieces = []
        for h in (2 * m, 2 * m + 1):
            x1 = qt[n_nope + h * QK_ROPE:n_nope + h * QK_ROPE + half]
            x2 = qt[n_nope + h * QK_ROPE + half:n_nope + (h + 1) * QK_ROPE]
            pieces += [x1 * cos - x2 * sin, x2 * cos + x1 * sin]
        roped = jnp.concatenate(pieces, axis=0).astype(BF16)
        for h in (2 * m, 2 * m + 1):
            qt_ref[h * QK_PAD:h * QK_PAD + QK_NOPE, :] = qt[h * QK_NOPE:(h + 1) * QK_NOPE].astype(BF16)
            qt_ref[h * QK_PAD + QK_NOPE:(h + 1) * QK_PAD, :] = roped


def _latent_q(h, pos_row, gpre, wd, gq, wut):
    t, d = h.shape
    tm = _tile(t, 512)
    row = lambda i: (i, 0)
    const = lambda i: (0, 0)
    freq = _rope_freq(1).reshape(QK_ROPE // 2, 1)
    scale = (QK_NOPE + QK_ROPE) ** -0.5 * LOG2_E
    return pl.pallas_call(
        functools.partial(_latent_q_body, scale=scale),
        grid=(t // tm,),
        in_specs=[
            pl.BlockSpec((tm, d), row),
            pl.BlockSpec((1, tm), lambda i: (0, i)),
            pl.BlockSpec((1, d), const),
            pl.BlockSpec(wd.shape, const),
            pl.BlockSpec((1, Q_LORA), const),
            pl.BlockSpec(wut.shape, const),
            pl.BlockSpec(freq.shape, const),
        ],
        out_specs=pl.BlockSpec((B_HEADS * QK_PAD, tm), lambda i: (0, i)),
        out_shape=jax.ShapeDtypeStruct((B_HEADS * QK_PAD, t), BF16),
        compiler_params=_params("parallel"),
        name="latent_q",
    )(h, pos_row, gpre, wd, gq, wut, freq)


HEADS_PER_STEP = 4


def _causal_attn_body(qt_ref, k_ref, vt_ref, o_ref, *, tq):
    qi = pl.program_id(2)

    def chunk(ki, carries, diagonal):
        start = pl.multiple_of(ki * tq, tq)
        out = []
        scores = []
        for hh in range(HEADS_PER_STEP):
            kc = slice(hh * QK_PAD, (hh + 1) * QK_PAD)
            scores.append(_dot(k_ref[pl.ds(start, tq), kc], qt_ref[kc, :]))
        for hh, (m, l, acc) in enumerate(carries):
            vr = slice(hh * V_DIM, (hh + 1) * V_DIM)
            st = scores[hh]
            if diagonal:
                key = lax.broadcasted_iota(jnp.int32, (tq, tq), 0)
                qry = lax.broadcasted_iota(jnp.int32, (tq, tq), 1)
                st = jnp.where(key <= qry, st, NEG)
            m_new = jnp.maximum(m, jnp.max(st, axis=0, keepdims=True))
            alpha = jnp.exp2(m - m_new)
            p = jnp.exp2(st - m_new)
            l = alpha * l + jnp.sum(p, axis=0, keepdims=True)
            acc = alpha * acc + _dot(vt_ref[vr, pl.ds(start, tq)], p.astype(BF16))
            out.append((m_new, l, acc))
        return tuple(out)

    init = tuple((jnp.full((1, tq), NEG, F32), jnp.zeros((1, tq), F32), jnp.zeros((V_DIM, tq), F32))
                 for _ in range(HEADS_PER_STEP))
    carries = lax.fori_loop(0, qi, lambda ki, c: chunk(ki, c, False), init)
    carries = chunk(qi, carries, True)
    for hh, (_, l, acc) in enumerate(carries):
        o_ref[:, hh * V_DIM:(hh + 1) * V_DIM] = jnp.transpose(acc / l).astype(o_ref.dtype)


def _causal_attention(qt, k, vt, b, s):
    t = b * s
    tq = _tile(s, 512)
    nq = s // tq
    hp = HEADS_PER_STEP
    return pl.pallas_call(
        functools.partial(_causal_attn_body, tq=tq),
        grid=(b, B_HEADS // hp, nq),
        in_specs=[
            pl.BlockSpec((hp * QK_PAD, tq), lambda bi, h, qi: (h, bi * nq + qi)),
            pl.BlockSpec((s, hp * QK_PAD), lambda bi, h, qi: (bi, h)),
            pl.BlockSpec((hp * V_DIM, s), lambda bi, h, qi: (h, bi)),
        ],
        out_specs=pl.BlockSpec((tq, hp * V_DIM), lambda bi, h, qi: (bi * nq + qi, h)),
        out_shape=jax.ShapeDtypeStruct((t, B_HEADS * V_DIM), BF16),
        compiler_params=_params("parallel", "parallel", "arbitrary"),
        name="causal_attn",
    )(qt, k, vt)


def kernel(x, p, positions, norms, ffn1_wg, ffn1_wu, ffn1_wd, ffn2_wg, ffn2_wu, ffn2_wd,
           ple_proj, ple_gate, a_wqkv, a_wo, b_wdq, b_q_norm, b_wuq, b_wo,
           kv_in_norm, w_dkv, kv_norm, w_ukv):
    b, s, d = x.shape
    depth = norms.shape[0]
    n_a = depth // 2
    t = b * s
    h = x.reshape(t, d)
    pos_col = positions.reshape(t, 1)
    row = lambda a: a.reshape(1, -1)
    shared = None
    ffn1 = [w.astype(BF16) for w in (ffn1_wg, ffn1_wu, ffn1_wd)]
    ffn2 = [w.astype(BF16) for w in (ffn2_wg, ffn2_wu, ffn2_wd)]
    for i in range(depth):
        g = [row(norms[i, n]) for n in range(N_NORMS)]
        h = _ffn(h, g[0], g[1], *ffn1, i)
        if i < n_a:
            qkv = _qkv_proj(h, g[2], a_wqkv[i].astype(BF16), b, s)
            outs = [_dilated_attention(qkv[grp], positions, grp, b, s) for grp in range(A_N_GROUPS)]
            h = _merge_proj([o for o, _ in outs], [l for _, l in outs], a_wo[i].astype(BF16), h, g[3], b, s)
        else:
            j = i - n_a
            w_uq = b_wuq[j].reshape(Q_LORA, B_HEADS, QK_NOPE + QK_ROPE)
            w_uqt = jnp.concatenate([w_uq[:, :, :QK_NOPE].reshape(Q_LORA, -1),
                                     w_uq[:, :, QK_NOPE:].reshape(Q_LORA, -1)], axis=1).T.astype(BF16)
            qt = _latent_q(h, positions.reshape(1, t), g[2], b_wdq[j].astype(BF16), row(b_q_norm[j]), w_uqt)
            o = _causal_attention(qt, shared[0], shared[1], b, s)
            h = _proj_res(o, b_wo[j].astype(BF16), h, g[3])
        h = _ffn(h, g[4], g[5], *ffn2, i)
        h = _ple(h, p[i].reshape(t, -1), g[6], g[7], ple_gate[i].astype(BF16), ple_proj[i].astype(BF16))
        if i == n_a - 1:
            shared = _latent_kv(h, pos_col, row(kv_in_norm), w_dkv.astype(BF16), row(kv_norm),
                                w_ukv.astype(BF16))
    return h.reshape(b, s, d)
```

```python
import functools

import numpy as np
import jax
import jax.numpy as jnp
from jax import lax
from jax.experimental import pallas as pl
from jax.experimental.pallas import tpu as pltpu

F32 = jnp.float32
BF16 = jnp.bfloat16

EPS = 1e-6
NEG = -1e30
N_NORMS = 8

A_GROUPS = ((128, 1), (512, 4), (2048, 16))
A_N_GROUPS = 3
A_HEADS = 8
A_HEAD_DIM = 128
A_OUT = A_HEADS * A_HEAD_DIM
BLOCK = 128
LOG2_E = float(np.log2(np.e))
A_Q_SCALE = A_HEAD_DIM ** -0.5 * LOG2_E

B_HEADS = 16
QK_NOPE = 128
QK_ROPE = 64
V_DIM = 128
Q_LORA = 512
KV_LORA = 512
ROPE_THETA = 10000.0
QK_PAD = 256

V7X_VMEM_BYTES = 64 * 1024 * 1024
VMEM_LIMIT_BYTES = V7X_VMEM_BYTES - 2 * 1024 * 1024
LANES = 128


def _tile(n, pref):
    if n <= pref:
        return n
    t = pref - pref % 8
    while t >= 8:
        if n % t == 0:
            return t
        t -= 8
    return n


def _params(*sem):
    return pltpu.CompilerParams(dimension_semantics=sem, vmem_limit_bytes=VMEM_LIMIT_BYTES)


def _rmsnorm(x, g):
    return x * lax.rsqrt(jnp.mean(x * x, axis=-1, keepdims=True) + EPS) * g


def _sigmoid(x):
    return 1.0 / (1.0 + jnp.exp(-x))


def _dot(a, b):
    return jnp.dot(a, b, preferred_element_type=F32)


def _dot_nt(a, b):
    return lax.dot_general(a, b, (((1,), (1,)), ((), ())), preferred_element_type=F32)


ROW_CHUNK = 32


def _for_row_chunks(n_rows, fn):
    def body(c, carry):
        fn(pl.ds(pl.multiple_of(c * ROW_CHUNK, ROW_CHUNK), ROW_CHUNK))
        return carry

    lax.fori_loop(0, n_rows // ROW_CHUNK, body, 0, unroll=True)


def _ffn_body(h_ref, gpre_ref, gpost_ref, wg_ref, wu_ref, wd_ref, o_ref, xn_ref):
    j = pl.program_id(1)
    tm, d = h_ref.shape

    @pl.when(j == 0)
    def _():
        gpre = gpre_ref[...]

        def pre(rows):
            xn_ref[rows, :] = _rmsnorm(h_ref[rows, :], gpre).astype(BF16)
            o_ref[rows, :] = jnp.zeros((ROW_CHUNK, d), F32)

        _for_row_chunks(tm, pre)

    xn = xn_ref[...]
    gate = _dot(xn, wg_ref[...])
    up = _dot(xn, wu_ref[...])
    act = (gate * _sigmoid(gate) * up).astype(BF16)
    o_ref[...] += _dot(act, wd_ref[...].astype(BF16))

    @pl.when(j == pl.num_programs(1) - 1)
    def _():
        half_gain = 0.5 * gpost_ref[...]

        def post(rows):
            o_ref[rows, :] = h_ref[rows, :] + _rmsnorm(o_ref[rows, :], half_gain)

        _for_row_chunks(tm, post)


def _ffn(h, gpre, gpost, wg, wu, wd, layer):
    t, d = h.shape
    f = wg.shape[2]
    tm = _tile(t, 1024)
    tf = _tile(f, 512)
    return pl.pallas_call(
        _ffn_body,
        grid=(t // tm, f // tf),
        in_specs=[
            pl.BlockSpec((tm, d), lambda i, j: (i, 0)),
            pl.BlockSpec((1, d), lambda i, j: (0, 0)),
            pl.BlockSpec((1, d), lambda i, j: (0, 0)),
            pl.BlockSpec((None, d, tf), lambda i, j: (layer, 0, j)),
            pl.BlockSpec((None, d, tf), lambda i, j: (layer, 0, j)),
            pl.BlockSpec((None, tf, d), lambda i, j: (layer, j, 0)),
        ],
        out_specs=pl.BlockSpec((tm, d), lambda i, j: (i, 0)),
        out_shape=jax.ShapeDtypeStruct((t, d), F32),
        scratch_shapes=[pltpu.VMEM((tm, d), BF16)],
        compiler_params=_params("parallel", "arbitrary"),
        name="ffn",
    )(h, gpre, gpost, wg, wu, wd)


def _qkv_proj_body(h_ref, g_ref, w_ref, o0_ref, o1_ref, o2_ref, xn_ref, slab_ref):
    j = pl.program_id(1)

    @pl.when(j == 0)
    def _():
        xn_ref[...] = _rmsnorm(h_ref[...], g_ref[...]).astype(BF16)

    res = _dot(xn_ref[...], w_ref[...])
    res = res * jnp.where(j % 3 == 0, A_Q_SCALE, 1.0)
    tm = res.shape[0]
    n_slab = res.shape[1] // LANES
    grp = j // 3

    @pl.when(grp == 0)
    def _():
        o0_ref[0, 0] = res.astype(BF16)

    @pl.when(grp > 0)
    def _():
        for sl in range(n_slab):
            slab_ref[sl] = res[:, sl * LANES:(sl + 1) * LANES]

    for o_ref, which in ((o1_ref, 1), (o2_ref, 2)):
        d = A_GROUPS[which][1]

        @pl.when(grp == which)
        def _(o_ref=o_ref, d=d):
            for r in range(d):
                for sl in range(n_slab):
                    o_ref[0, r, :, sl * LANES:(sl + 1) * LANES] = (
                        slab_ref[sl, pl.ds(r, tm // d, stride=d), :].astype(BF16))


def _qkv_proj(h, g, w, b, s):
    t, dm = h.shape
    tm = _tile(s, 1024)
    n_chunk = s // tm
    n_j = 3 * A_N_GROUPS

    def out_spec(which):
        d = A_GROUPS[which][1]
        return pl.BlockSpec(
            (1, d, tm // d, A_OUT),
            lambda i, j: (i // n_chunk, 0, i % n_chunk, jnp.clip(j - 3 * which, 0, 2)))

    return pl.pallas_call(
        _qkv_proj_body,
        grid=(t // tm, n_j),
        in_specs=[
            pl.BlockSpec((tm, dm), lambda i, j: (i, 0)),
            pl.BlockSpec((1, dm), lambda i, j: (0, 0)),
            pl.BlockSpec((dm, A_OUT), lambda i, j: (0, (j % 3) * A_N_GROUPS + j // 3)),
        ],
        out_specs=[out_spec(0), out_spec(1), out_spec(2)],
        out_shape=[jax.ShapeDtypeStruct((b, d, s // d, 3 * A_OUT), BF16) for _, d in A_GROUPS],
        scratch_shapes=[pltpu.VMEM((tm, dm), BF16), pltpu.VMEM((A_OUT // LANES, tm, LANES), F32)],
        compiler_params=_params("parallel", "arbitrary"),
        name="qkv_proj",
    )(h, g, w)


MASKED_DIST = 1e30


BLOCKS_PER_STEP = 4


def _dilated_body(q_ref, kc_ref, kp_ref, vc_ref, vp_ref, pc_ref, pp_ref, o_ref, lse_ref,
                  *, slopes, sub_w, n_sub):
    step = pl.program_id(2)
    k_all = jnp.concatenate([kp_ref[0, 0], kc_ref[0, 0]], axis=0)
    v_all = jnp.concatenate([vp_ref[0, 0], vc_ref[0, 0]], axis=0)
    p_all = jnp.concatenate([pp_ref[0], pc_ref[0]], axis=1).astype(F32)

    qi = lax.broadcasted_iota(jnp.int32, (BLOCK, 2 * BLOCK), 0)
    kj = lax.broadcasted_iota(jnp.int32, (BLOCK, 2 * BLOCK), 1)
    diff = BLOCK + qi - kj
    band = (diff >= 0) & (diff <= sub_w)
    first_key = jnp.where(step > 0, 0, BLOCK)
    head_lane = lax.broadcasted_iota(jnp.int32, (BLOCK, LANES), 1)
    head_cols = [slice(h * A_HEAD_DIM, (h + 1) * A_HEAD_DIM) for h in range(A_HEADS)]

    for i in range(n_sub):
        rows = slice(i * BLOCK, (i + 1) * BLOCK)
        q = q_ref[0, 0, rows, :]
        k = k_all[i * BLOCK:(i + 2) * BLOCK]
        v = v_all[i * BLOCK:(i + 2) * BLOCK]
        pk = p_all[:, i * BLOCK:(i + 2) * BLOCK]
        pq = p_all[:, (i + 1) * BLOCK:(i + 2) * BLOCK]
        pq_col = jnp.transpose(jnp.broadcast_to(pq, (BLOCK, BLOCK)))
        dist = jnp.abs(jnp.concatenate([pq_col, pq_col], axis=1) - pk)
        mask = band & (kj >= first_key) if i == 0 else band
        dist = jnp.where(mask, dist, MASKED_DIST)

        lse_tile = jnp.zeros((BLOCK, LANES), F32)
        scores = [_dot_nt(q[:, cols], k[:, cols]) for cols in head_cols]
        for h in range(A_HEADS):
            cols = head_cols[h]
            s = scores[h] - slopes[h] * dist
            m = jnp.max(s, axis=-1, keepdims=True)
            p = jnp.exp2(s - m)
            l = jnp.sum(p, axis=-1, keepdims=True)
            o_ref[0, 0, rows, cols] = _dot(p.astype(BF16), v[:, cols]) / l
            lse_tile = jnp.where(head_lane == h, m + jnp.log2(l), lse_tile)
        lse_ref[0, 0, rows, :] = lse_tile


def _dilated_attention(qkv_g, pos, g, b, s):
    window, d = A_GROUPS[g]
    l = s // d
    n_sub = min(BLOCKS_PER_STEP, l // BLOCK)
    rows = n_sub * BLOCK
    pos_s = pos.reshape(b, l, d).transpose(0, 2, 1).reshape(b * d, 1, l)
    slopes = tuple(float(2.0 ** (-8.0 * (g * A_HEADS + h + 1) / (A_N_GROUPS * A_HEADS))) * LOG2_E
                   for h in range(A_HEADS))

    def cur(c):
        return lambda bi, r, st: (bi, r, st, c)

    def prev(c):
        return lambda bi, r, st: (bi, r, jnp.maximum(st * n_sub - 1, 0), c)

    cur_spec = (1, 1, rows, A_OUT)
    prev_spec = (1, 1, BLOCK, A_OUT)
    return pl.pallas_call(
        functools.partial(_dilated_body, slopes=slopes, sub_w=window // d, n_sub=n_sub),
        grid=(b, d, l // rows),
        in_specs=[
            pl.BlockSpec(cur_spec, cur(0)),
            pl.BlockSpec(cur_spec, cur(1)),
            pl.BlockSpec(prev_spec, prev(1)),
            pl.BlockSpec(cur_spec, cur(2)),
            pl.BlockSpec(prev_spec, prev(2)),
            pl.BlockSpec((1, 1, rows), lambda bi, r, st: (bi * d + r, 0, st)),
            pl.BlockSpec((1, 1, BLOCK), lambda bi, r, st: (bi * d + r, 0, jnp.maximum(st * n_sub - 1, 0))),
        ],
        out_specs=[
            pl.BlockSpec(cur_spec, cur(0)),
            pl.BlockSpec((1, 1, rows, LANES), cur(0)),
        ],
        out_shape=[
            jax.ShapeDtypeStruct((b, d, l, A_OUT), F32),
            jax.ShapeDtypeStruct((b, d, l, LANES), F32),
        ],
        compiler_params=_params("parallel", "parallel", "arbitrary"),
        name=f"dilated_attn_g{g}",
    )(qkv_g, qkv_g, qkv_g, qkv_g, qkv_g, pos_s, pos_s)


def _merge_proj_body(o0_ref, o1_ref, o2_ref, l0_ref, l1_ref, l2_ref, w_ref, h_ref, g_ref, out_ref,
                     os1_ref, os2_ref, ls1_ref, ls2_ref):
    tm = out_ref.shape[0]
    for o_ref, l_ref, os_ref, ls_ref, which in ((o1_ref, l1_ref, os1_ref, ls1_ref, 1),
                                                (o2_ref, l2_ref, os2_ref, ls2_ref, 2)):
        d = A_GROUPS[which][1]
        for r in range(d):
            rows = pl.ds(r, tm // d, stride=d)
            ls_ref[rows, :] = l_ref[0, r]
            for h in range(A_HEADS):
                os_ref[h, rows, :] = o_ref[0, r, :, h * A_HEAD_DIM:(h + 1) * A_HEAD_DIM]

    l0, l1, l2 = l0_ref[0, 0], ls1_ref[...], ls2_ref[...]
    mx = jnp.maximum(jnp.maximum(l0, l1), l2)
    e0, e1, e2 = jnp.exp2(l0 - mx), jnp.exp2(l1 - mx), jnp.exp2(l2 - mx)
    inv = 1.0 / (e0 + e1 + e2)
    w0, w1, w2 = e0 * inv, e1 * inv, e2 * inv
    parts = []
    for h in range(A_HEADS):
        cols = slice(h * A_HEAD_DIM, (h + 1) * A_HEAD_DIM)
        parts.append(w0[:, h:h + 1] * o0_ref[0, 0, :, cols] + w1[:, h:h + 1] * os1_ref[h]
                     + w2[:, h:h + 1] * os2_ref[h])
    merged = jnp.concatenate(parts, axis=1).astype(BF16)
    out_ref[...] = h_ref[...] + _rmsnorm(_dot(merged, w_ref[...]), g_ref[...])


def _merge_proj(os, lses, w, h, g, b, s):
    t, dm = h.shape
    tm = _tile(s, 512)
    n_chunk = s // tm
    row = lambda i: (i, 0)
    const = lambda i: (0, 0)
    grp = lambda i: (i // n_chunk, 0, i % n_chunk, 0)
    o_specs = [pl.BlockSpec((1, d, tm // d, A_OUT), grp) for _, d in A_GROUPS]
    l_specs = [pl.BlockSpec((1, d, tm // d, LANES), grp) for _, d in A_GROUPS]
    return pl.pallas_call(
        _merge_proj_body,
        grid=(t // tm,),
        in_specs=o_specs + l_specs + [
            pl.BlockSpec(w.shape, const),
            pl.BlockSpec((tm, dm), row),
            pl.BlockSpec((1, dm), const),
        ],
        out_specs=pl.BlockSpec((tm, dm), row),
        out_shape=jax.ShapeDtypeStruct((t, dm), F32),
        scratch_shapes=[pltpu.VMEM((A_HEADS, tm, A_HEAD_DIM), F32), pltpu.VMEM((A_HEADS, tm, A_HEAD_DIM), F32),
                        pltpu.VMEM((tm, LANES), F32), pltpu.VMEM((tm, LANES), F32)],
        compiler_params=_params("parallel"),
        name="merge_proj",
    )(*os, *lses, w, h, g)


def _proj_res_body(x_ref, w_ref, h_ref, g_ref, out_ref):
    out_ref[...] = h_ref[...] + _rmsnorm(_dot(x_ref[...], w_ref[...]), g_ref[...])


def _proj_res(x, w, h, g):
    t, d = h.shape
    tm = _tile(t, 512)
    row = lambda i: (i, 0)
    const = lambda i: (0, 0)
    return pl.pallas_call(
        _proj_res_body,
        grid=(t // tm,),
        in_specs=[
            pl.BlockSpec((tm, x.shape[1]), row),
            pl.BlockSpec(w.shape, const),
            pl.BlockSpec((tm, d), row),
            pl.BlockSpec((1, d), const),
        ],
        out_specs=pl.BlockSpec((tm, d), row),
        out_shape=jax.ShapeDtypeStruct((t, d), F32),
        compiler_params=_params("parallel"),
        name="proj_res",
    )(x, w, h, g)


def _ple_body(h_ref, p_ref, gpre_ref, gpost_ref, wgate_ref, wproj_ref, out_ref):
    h = h_ref[...]
    gate = _sigmoid(_dot(_rmsnorm(h, gpre_ref[...]).astype(BF16), wgate_ref[...]))
    emb = _dot(p_ref[...].astype(BF16), wproj_ref[...])
    out_ref[...] = h + _rmsnorm(emb * gate, gpost_ref[...])


def _ple(h, p, gpre, gpost, wgate, wproj):
    t, d = h.shape
    tm = _tile(t, 512)
    row = lambda i: (i, 0)
    const = lambda i: (0, 0)
    return pl.pallas_call(
        _ple_body,
        grid=(t // tm,),
        in_specs=[
            pl.BlockSpec((tm, d), row),
            pl.BlockSpec((tm, p.shape[1]), row),
            pl.BlockSpec((1, d), const),
            pl.BlockSpec((1, d), const),
            pl.BlockSpec(wgate.shape, const),
            pl.BlockSpec(wproj.shape, const),
        ],
        out_specs=pl.BlockSpec((tm, d), row),
        out_shape=jax.ShapeDtypeStruct((t, d), F32),
        compiler_params=_params("parallel"),
        name="ple",
    )(h, p, gpre, gpost, wgate, wproj)


def _rope_freq(repeats):
    inv = ROPE_THETA ** (-np.arange(0, QK_ROPE, 2) / QK_ROPE)
    return jnp.asarray(np.tile(inv, repeats), F32)


def _latent_kv_body(h_ref, pos_ref, gin_ref, wd_ref, gkv_ref, wu_ref, freq_ref, k_ref, vt_ref):
    xn = _rmsnorm(h_ref[...], gin_ref[...]).astype(BF16)
    ckr = _dot(xn, wd_ref[...])
    c_kv = _rmsnorm(ckr[:, :KV_LORA], gkv_ref[...]).astype(BF16)
    kv = _dot(c_kv, wu_ref[...])

    x = ckr[:, KV_LORA:]
    ang = pos_ref[...].astype(F32) * freq_ref[...]
    half = QK_ROPE // 2
    rot = jnp.concatenate([-x[:, half:], x[:, :half]], axis=1)
    k_rope = x * jnp.cos(ang) + rot * jnp.sin(ang)
    zeros = jnp.zeros_like(k_rope)
    slot_a = jnp.concatenate([k_rope, zeros], axis=1).astype(BF16)
    slot_b = jnp.concatenate([zeros, k_rope], axis=1).astype(BF16)

    per_head = QK_NOPE + V_DIM
    for h in range(B_HEADS):
        k_ref[:, h * QK_PAD:h * QK_PAD + QK_NOPE] = kv[:, h * per_head:h * per_head + QK_NOPE].astype(BF16)
        k_ref[:, h * QK_PAD + QK_NOPE:(h + 1) * QK_PAD] = slot_a if h % 2 == 0 else slot_b
        v_h = kv[:, h * per_head + QK_NOPE:(h + 1) * per_head]
        vt_ref[h * V_DIM:(h + 1) * V_DIM, :] = jnp.transpose(v_h).astype(BF16)


def _latent_kv(h, pos_col, gin, wd, gkv, wu):
    t, d = h.shape
    tm = _tile(t, 512)
    row = lambda i: (i, 0)
    const = lambda i: (0, 0)
    freq = _rope_freq(2).reshape(1, QK_ROPE)
    return pl.pallas_call(
        _latent_kv_body,
        grid=(t // tm,),
        in_specs=[
            pl.BlockSpec((tm, d), row),
            pl.BlockSpec((tm, 1), row),
            pl.BlockSpec((1, d), const),
            pl.BlockSpec(wd.shape, const),
            pl.BlockSpec((1, KV_LORA), const),
            pl.BlockSpec(wu.shape, const),
            pl.BlockSpec(freq.shape, const),
        ],
        out_specs=[
            pl.BlockSpec((tm, B_HEADS * QK_PAD), row),
            pl.BlockSpec((B_HEADS * V_DIM, tm), lambda i: (0, i)),
        ],
        out_shape=[
            jax.ShapeDtypeStruct((t, B_HEADS * QK_PAD), BF16),
            jax.ShapeDtypeStruct((B_HEADS * V_DIM, t), BF16),
        ],
        compiler_params=_params("parallel"),
        name="latent_kv",
    )(h, pos_col, gin, wd, gkv, wu, freq)


def _latent_q_body(h_ref, pos_ref, gpre_ref, wd_ref, gq_ref, wut_ref, freq_ref, qt_ref, *, scale):
    xn = _rmsnorm(h_ref[...], gpre_ref[...]).astype(BF16)
    c_q = _rmsnorm(_dot(xn, wd_ref[...]), gq_ref[...])
    qt = _dot(wut_ref[...], jnp.transpose(c_q).astype(BF16)) * scale
    ang = freq_ref[...] * pos_ref[...].astype(F32)
    cos, sin = jnp.cos(ang), jnp.sin(ang)
    n_nope = B_HEADS * QK_NOPE
    half = QK_ROPE // 2
    for m in range(B_HEADS // 2):
        pieces = []
        for h in (2 * m, 2 * m + 1):
            x1 = qt[n_nope + h * QK_ROPE:n_nope + h * QK_ROPE + half]
            x2 = qt[n_nope + h * QK_ROPE + half:n_nope + (h + 1) * QK_ROPE]
            pieces += [x1 * cos - x2 * sin, x2 * cos + x1 * sin]
        roped = jnp.concatenate(pieces, axis=0).astype(BF16)
        for h in (2 * m, 2 * m + 1):
            qt_ref[h * QK_PAD:h * QK_PAD + QK_NOPE, :] = qt[h * QK_NOPE:(h + 1) * QK_NOPE].astype(BF16)
            qt_ref[h * QK_PAD + QK_NOPE:(h + 1) * QK_PAD, :] = roped


def _latent_q(h, pos_row, gpre, wd, gq, wut):
    t, d = h.shape
    tm = _tile(t, 512)
    row = lambda i: (i, 0)
    const = lambda i: (0, 0)
    freq = _rope_freq(1).reshape(QK_ROPE // 2, 1)
    scale = (QK_NOPE + QK_ROPE) ** -0.5 * LOG2_E
    return pl.pallas_call(
        functools.partial(_latent_q_body, scale=scale),
        grid=(t // tm,),
        in_specs=[
            pl.BlockSpec((tm, d), row),
            pl.BlockSpec((1, tm), lambda i: (0, i)),
            pl.BlockSpec((1, d), const),
            pl.BlockSpec(wd.shape, const),
            pl.BlockSpec((1, Q_LORA), const),
            pl.BlockSpec(wut.shape, const),
            pl.BlockSpec(freq.shape, const),
        ],
        out_specs=pl.BlockSpec((B_HEADS * QK_PAD, tm), lambda i: (0, i)),
        out_shape=jax.ShapeDtypeStruct((B_HEADS * QK_PAD, t), BF16),
        compiler_params=_params("parallel"),
        name="latent_q",
    )(h, pos_row, gpre, wd, gq, wut, freq)


HEADS_PER_STEP = 4


def _causal_attn_body(qt_ref, k_ref, vt_ref, o_ref, *, tq):
    qi = pl.program_id(2)

    def chunk(ki, carries, diagonal):
        start = pl.multiple_of(ki * tq, tq)
        out = []
        scores = []
        for hh in range(HEADS_PER_STEP):
            kc = slice(hh * QK_PAD, (hh + 1) * QK_PAD)
            scores.append(_dot(k_ref[pl.ds(start, tq), kc], qt_ref[kc, :]))
        for hh, (m, l, acc) in enumerate(carries):
            vr = slice(hh * V_DIM, (hh + 1) * V_DIM)
            st = scores[hh]
            if diagonal:
                key = lax.broadcasted_iota(jnp.int32, (tq, tq), 0)
                qry = lax.broadcasted_iota(jnp.int32, (tq, tq), 1)
                st = jnp.where(key <= qry, st, NEG)
            m_new = jnp.maximum(m, jnp.max(st, axis=0, keepdims=True))
            alpha = jnp.exp2(m - m_new)
            p = jnp.exp2(st - m_new)
            l = alpha * l + jnp.sum(p, axis=0, keepdims=True)
            acc = alpha * acc + _dot(vt_ref[vr, pl.ds(start, tq)], p.astype(BF16))
            out.append((m_new, l, acc))
        return tuple(out)

    init = tuple((jnp.full((1, tq), NEG, F32), jnp.zeros((1, tq), F32), jnp.zeros((V_DIM, tq), F32))
                 for _ in range(HEADS_PER_STEP))
    carries = lax.fori_loop(0, qi, lambda ki, c: chunk(ki, c, False), init)
    carries = chunk(qi, carries, True)
    for hh, (_, l, acc) in enumerate(carries):
        o_ref[:, hh * V_DIM:(hh + 1) * V_DIM] = jnp.transpose(acc / l).astype(o_ref.dtype)


def _causal_attention(qt, k, vt, b, s):
    t = b * s
    tq = _tile(s, 512)
    nq = s // tq
    hp = HEADS_PER_STEP
    return pl.pallas_call(
        functools.partial(_causal_attn_body, tq=tq),
        grid=(b, B_HEADS // hp, nq),
        in_specs=[
            pl.BlockSpec((hp * QK_PAD, tq), lambda bi, h, qi: (h, bi * nq + qi)),
            pl.BlockSpec((s, hp * QK_PAD), lambda bi, h, qi: (bi, h)),
            pl.BlockSpec((hp * V_DIM, s), lambda bi, h, qi: (h, bi)),
        ],
        out_specs=pl.BlockSpec((tq, hp * V_DIM), lambda bi, h, qi: (bi * nq + qi, h)),
        out_shape=jax.ShapeDtypeStruct((t, B_HEADS * V_DIM), BF16),
        compiler_params=_params("parallel", "parallel", "arbitrary"),
        name="causal_attn",
    )(qt, k, vt)


def kernel(x, p, positions, norms, ffn1_wg, ffn1_wu, ffn1_wd, ffn2_wg, ffn2_wu, ffn2_wd,
           ple_proj, ple_gate, a_wqkv, a_wo, b_wdq, b_q_norm, b_wuq, b_wo,
           kv_in_norm, w_dkv, kv_norm, w_ukv):
    b, s, d = x.shape
    depth = norms.shape[0]
    n_a = depth // 2
    t = b * s
    h = x.reshape(t, d)
    pos_col = positions.reshape(t, 1)
    row = lambda a: a.reshape(1, -1)
    shared = None
    ffn1 = (ffn1_wg.astype(BF16), ffn1_wu.astype(BF16), ffn1_wd)
    ffn2 = (ffn2_wg.astype(BF16), ffn2_wu.astype(BF16), ffn2_wd)
    for i in range(depth):
        g = [row(norms[i, n]) for n in range(N_NORMS)]
        h = _ffn(h, g[0], g[1], *ffn1, i)
        if i < n_a:
            qkv = _qkv_proj(h, g[2], a_wqkv[i].astype(BF16), b, s)
            outs = [_dilated_attention(qkv[grp], positions, grp, b, s) for grp in range(A_N_GROUPS)]
            h = _merge_proj([o for o, _ in outs], [l for _, l in outs], a_wo[i].astype(BF16), h, g[3], b, s)
        else:
            j = i - n_a
            w_uq = b_wuq[j].reshape(Q_LORA, B_HEADS, QK_NOPE + QK_ROPE)
            w_uqt = jnp.concatenate([w_uq[:, :, :QK_NOPE].reshape(Q_LORA, -1),
                                     w_uq[:, :, QK_NOPE:].reshape(Q_LORA, -1)], axis=1).T.astype(BF16)
            qt = _latent_q(h, positions.reshape(1, t), g[2], b_wdq[j].astype(BF16), row(b_q_norm[j]), w_uqt)
            o = _causal_attention(qt, shared[0], shared[1], b, s)
            h = _proj_res(o, b_wo[j].astype(BF16), h, g[3])
        h = _ffn(h, g[4], g[5], *ffn2, i)
        h = _ple(h, p[i].reshape(t, -1), g[6], g[7], ple_gate[i].astype(BF16), ple_proj[i].astype(BF16))
        if i == n_a - 1:
            shared = _latent_kv(h, pos_col, row(kv_in_norm), w_dkv.astype(BF16), row(kv_norm),
                                w_ukv.astype(BF16))
    return h.reshape(b, s, d)
```
